```python
import math
import jax, jax.numpy as jnp
from jax import lax
import numpy as np

D_MODEL = 2048
BATCH = 4
SEQ = 8192
DEPTH = 4

CHUNK = 64
N_META = 16
Q_BLOCK = 128
NORM_EPS = 1e-6
NEG_INF = -1e30

CONV_WIDTH = 1024
CONV_KERNEL = 31

DIFF_HEADS = 8
DIFF_HEAD_DIM = 64
DIFF_V_DIM = 2 * DIFF_HEAD_DIM
REL_BUCKETS = 32
REL_MAX_DIST = 128

MLA_HEADS = 8
MLA_Q_RANK = 512
MLA_KV_RANK = 256
MLA_NOPE_DIM = 128
MLA_ROPE_DIM = 64
MLA_V_DIM = 128
ROPE_THETA = 10000.0

D_FF = -(-8 * D_MODEL // (3 * 256)) * 256

N_BRANCH = 3
OFF_A = 0
OFF_BQ = OFF_A + 2 * CONV_WIDTH
DIFF_QK = DIFF_HEADS * 2 * DIFF_HEAD_DIM
OFF_BK = OFF_BQ + DIFF_QK
OFF_BV = OFF_BK + DIFF_QK
OFF_CQ = OFF_BV + DIFF_HEADS * DIFF_V_DIM
OFF_CKV = OFF_CQ + MLA_Q_RANK
OFF_CR = OFF_CKV + MLA_KV_RANK
OFF_G = OFF_CR + MLA_ROPE_DIM
IN_COLS = OFF_G + N_BRANCH * D_MODEL

kernel_name = "hybrid_gated_conv_diffattn_mla_block"


def rmsnorm(x, g):
    xf = x.astype(jnp.float32)
    y = xf * lax.rsqrt(jnp.mean(xf * xf, axis=-1, keepdims=True) + NORM_EPS)
    return (y * g.astype(jnp.float32)).astype(x.dtype)


def layernorm(x, g, b):
    xf = x.astype(jnp.float32)
    mu = jnp.mean(xf, axis=-1, keepdims=True)
    var = jnp.mean(jnp.square(xf - mu), axis=-1, keepdims=True)
    y = (xf - mu) * lax.rsqrt(var + NORM_EPS)
    return (y * g.astype(jnp.float32) + b.astype(jnp.float32)).astype(x.dtype)


def heads(t, n, d):
    b, l, _ = t.shape
    return t.reshape(b, l, n, d).transpose(0, 2, 1, 3)


def merge_heads(t):
    b, h, l, d = t.shape
    return t.transpose(0, 2, 1, 3).reshape(b, l, h * d)


def chunk_id(pos):
    return jnp.where(pos < N_META, 0, (pos - N_META) // CHUNK + 1)


def t5_bucket(rel):
    nb = REL_BUCKETS // 2
    ret = jnp.where(rel > 0, nb, 0)
    n = jnp.abs(rel)
    max_exact = nb // 2
    nf = jnp.maximum(n, 1).astype(jnp.float32)
    large = max_exact + (jnp.log(nf / max_exact) / math.log(REL_MAX_DIST / max_exact)
                         * (nb - max_exact)).astype(jnp.int32)
    large = jnp.minimum(large, nb - 1)
    return ret + jnp.where(n < max_exact, n, large)


def rope(x, pos):
    half = MLA_ROPE_DIM // 2
    inv = ROPE_THETA ** (-jnp.arange(half, dtype=jnp.float32) / half)
    ang = pos.astype(jnp.float32)[:, None] * inv[None, :]
    cos, sin = jnp.cos(ang), jnp.sin(ang)
    xf = x.astype(jnp.float32)
    x1, x2 = xf[..., :half], xf[..., half:]
    return jnp.concatenate([x1 * cos - x2 * sin, x2 * cos + x1 * sin], axis=-1).astype(x.dtype)


def sweep_queries(fn, qs, pos):
    meta_out = fn(tuple(q[:, :, :N_META] for q in qs), pos[:N_META])
    n_real = pos.shape[0] - N_META
    nblk = n_real // Q_BLOCK

    def split(q):
        b, h, _, d = q.shape
        return jnp.moveaxis(q[:, :, N_META:].reshape(b, h, nblk, Q_BLOCK, d), 2, 0)

    blocks = tuple(split(q) for q in qs)
    pos_blk = pos[N_META:].reshape(nblk, Q_BLOCK)
    out = lax.map(lambda a: fn(a[0], a[1]), (blocks, pos_blk))
    nb, b, h, qb, dv = out.shape
    out = jnp.moveaxis(out, 0, 2).reshape(b, h, nb * qb, dv)
    return jnp.concatenate([meta_out, out], axis=2)


def causal_depthwise_conv(u, w, bias):
    out = lax.conv_general_dilated(
        u, w[:, None, :].astype(u.dtype), window_strides=(1,),
        padding=((CONV_KERNEL - 1, 0),),
        dimension_numbers=("NWC", "WIO", "NWC"),
        feature_group_count=u.shape[-1])
    return out + bias


def conv_branch(proj, conv_w, conv_b, ln_g, ln_b):
    val = proj[..., OFF_A:OFF_A + CONV_WIDTH]
    gate = proj[..., OFF_A + CONV_WIDTH:OFF_BQ]
    a = val * jax.nn.sigmoid(gate)
    a = causal_depthwise_conv(a, conv_w, conv_b)
    return jax.nn.silu(layernorm(a, ln_g, ln_b))


def diff_branch(proj, pos, rel_table, lq1, lk1, lq2, lk2, subln_g, lam_init):
    q = heads(proj[..., OFF_BQ:OFF_BK], 2 * DIFF_HEADS, DIFF_HEAD_DIM)
    k = heads(proj[..., OFF_BK:OFF_BV], 2 * DIFF_HEADS, DIFF_HEAD_DIM)
    v = heads(proj[..., OFF_BV:OFF_CQ], DIFF_HEADS, DIFF_V_DIM)
    q1, q2 = q[:, 0::2], q[:, 1::2]
    k1, k2 = k[:, 0::2], k[:, 1::2]
    f32 = jnp.float32
    lam = (jnp.exp(jnp.sum(lq1.astype(f32) * lk1.astype(f32)))
           - jnp.exp(jnp.sum(lq2.astype(f32) * lk2.astype(f32))) + lam_init)
    scale = DIFF_HEAD_DIM ** -0.5
    k_chunk = chunk_id(pos)

    def attend(qb, q_pos):
        qa, qc = qb
        vis = k_chunk[None, :] <= chunk_id(q_pos)[:, None]
        bias = jnp.transpose(rel_table[t5_bucket(pos[None, :] - q_pos[:, None])],
                             (2, 0, 1)).astype(f32)

        def probs(qq, kk):
            s = jnp.einsum("bhqd,bhkd->bhqk", qq, kk).astype(f32) * scale + bias
            return jax.nn.softmax(jnp.where(vis, s, NEG_INF), axis=-1)

        p = probs(qa, k1) - lam * probs(qc, k2)
        return jnp.einsum("bhqk,bhkd->bhqd", p.astype(v.dtype), v)

    o = sweep_queries(attend, (q1, q2), pos)
    o = rmsnorm(o, subln_g) * (1.0 - lam_init)
    return merge_heads(o)


def mla_branch(proj, pos, q_norm, w_uq, kv_norm, w_ukv):
    cq = rmsnorm(proj[..., OFF_CQ:OFF_CKV], q_norm)
    q = heads(cq @ w_uq, MLA_HEADS, MLA_NOPE_DIM + MLA_ROPE_DIM)
    q_nope = q[..., :MLA_NOPE_DIM]
    q_rope = rope(q[..., MLA_NOPE_DIM:], pos)
    ckv = rmsnorm(proj[..., OFF_CKV:OFF_CR], kv_norm)
    kv = heads(ckv @ w_ukv, MLA_HEADS, MLA_NOPE_DIM + MLA_V_DIM)
    k_nope = kv[..., :MLA_NOPE_DIM]
    v = kv[..., MLA_NOPE_DIM:]
    k_rope = rope(proj[..., OFF_CR:OFF_G], pos)
    scale = (MLA_NOPE_DIM + MLA_ROPE_DIM) ** -0.5
    k_chunk = chunk_id(pos)

    def attend(qb, q_pos):
        qn, qr = qb
        vis = k_chunk[None, :] <= chunk_id(q_pos)[:, None]
        s = (jnp.einsum("bhqd,bhkd->bhqk", qn, k_nope)
             + jnp.einsum("bhqr,bkr->bhqk", qr, k_rope)).astype(jnp.float32) * scale
        p = jax.nn.softmax(jnp.where(vis, s, NEG_INF), axis=-1)
        return jnp.einsum("bhqk,bhkd->bhqd", p.astype(v.dtype), v)

    return merge_heads(sweep_queries(attend, (q_nope, q_rope), pos))


def setup_inputs(seed: int = 0) -> dict:
    key = jax.random.key(seed)
    ks = jax.random.split(key, 32)
    f32 = jnp.float32

    def w(k, shape, fan_in):
        return jax.random.normal(k, shape, f32) * (fan_in ** -0.5)

    def gain(k, shape):
        return 1.0 + 0.02 * jax.random.normal(k, shape, f32)

    L = DEPTH
    return {
        "x": jax.random.normal(ks[0], (BATCH, SEQ, D_MODEL), f32),
        "meta_tokens": jax.random.normal(ks[1], (N_META, D_MODEL), f32),
        "rel_bias": 0.5 * jax.random.normal(ks[2], (REL_BUCKETS, DIFF_HEADS), f32),
        "norm_mix": gain(ks[3], (L, D_MODEL)),
        "w_in": w(ks[4], (L, D_MODEL, IN_COLS), D_MODEL),
        "conv_w": w(ks[5], (L, CONV_KERNEL, CONV_WIDTH), CONV_KERNEL),
        "conv_b": 0.01 * jax.random.normal(ks[6], (L, CONV_WIDTH), f32),
        "conv_ln_g": gain(ks[7], (L, CONV_WIDTH)),
        "conv_ln_b": 0.01 * jax.random.normal(ks[8], (L, CONV_WIDTH), f32),
        "w_br_conv": w(ks[9], (L, CONV_WIDTH, D_MODEL), CONV_WIDTH),
        "diff_lq1": 0.1 * jax.random.normal(ks[10], (L, DIFF_HEAD_DIM), f32),
        "diff_lk1": 0.1 * jax.random.normal(ks[11], (L, DIFF_HEAD_DIM), f32),
        "diff_lq2": 0.1 * jax.random.normal(ks[12], (L, DIFF_HEAD_DIM), f32),
        "diff_lk2": 0.1 * jax.random.normal(ks[13], (L, DIFF_HEAD_DIM), f32),
        "diff_subln": gain(ks[14], (L, DIFF_V_DIM)),
        "w_br_diff": w(ks[15], (L, DIFF_HEADS * DIFF_V_DIM, D_MODEL), DIFF_HEADS * DIFF_V_DIM),
        "mla_q_norm": gain(ks[16], (L, MLA_Q_RANK)),
        "w_uq": w(ks[17], (L, MLA_Q_RANK, MLA_HEADS * (MLA_NOPE_DIM + MLA_ROPE_DIM)), MLA_Q_RANK),
        "mla_kv_norm": gain(ks[18], (L, MLA_KV_RANK)),
        "w_ukv": w(ks[19], (L, MLA_KV_RANK, MLA_HEADS * (MLA_NOPE_DIM + MLA_V_DIM)), MLA_KV_RANK),
        "w_br_mla": w(ks[20], (L, MLA_HEADS * MLA_V_DIM, D_MODEL), MLA_HEADS * MLA_V_DIM),
        "w_out": w(ks[21], (L, D_MODEL, D_MODEL), D_MODEL),
        "norm_ffn": gain(ks[22], (L, D_MODEL)),
        "w_ffn_gate": w(ks[23], (L, D_MODEL, D_FF), D_MODEL),
        "w_ffn_up": w(ks[24], (L, D_MODEL, D_FF), D_MODEL),
        "w_ffn_down": w(ks[25], (L, D_FF, D_MODEL), D_FF),
        "final_norm": gain(ks[26], (D_MODEL,)),
    }


def reference(x, meta_tokens, rel_bias, norm_mix, w_in, conv_w, conv_b, conv_ln_g, conv_ln_b,
              w_br_conv, diff_lq1, diff_lk1, diff_lq2, diff_lk2, diff_subln, w_br_diff,
              mla_q_norm, w_uq, mla_kv_norm, w_ukv, w_br_mla, w_out, norm_ffn,
              w_ffn_gate, w_ffn_up, w_ffn_down, final_norm):
    b = x.shape[0]
    h = jnp.concatenate(
        [jnp.broadcast_to(meta_tokens[None].astype(x.dtype), (b, N_META, D_MODEL)), x], axis=1)
    L = h.shape[1]
    pos = jnp.arange(L, dtype=jnp.int32)
    for l in range(DEPTH):
        lam_init = 0.8 - 0.6 * math.exp(-0.3 * l)
        hn = rmsnorm(h, norm_mix[l])
        proj = hn @ w_in[l]
        ya = conv_branch(proj, conv_w[l], conv_b[l], conv_ln_g[l], conv_ln_b[l]) @ w_br_conv[l]
        yb = diff_branch(proj, pos, rel_bias, diff_lq1[l], diff_lk1[l], diff_lq2[l], diff_lk2[l],
                         diff_subln[l], lam_init) @ w_br_diff[l]
        yc = mla_branch(proj, pos, mla_q_norm[l], w_uq[l], mla_kv_norm[l], w_ukv[l]) @ w_br_mla[l]
        g = jax.nn.sigmoid(proj[..., OFF_G:]).reshape(b, L, N_BRANCH, D_MODEL)
        mixed = g[:, :, 0] * ya + g[:, :, 1] * yb + g[:, :, 2] * yc
        h = h + mixed @ w_out[l]
        hn = rmsnorm(h, norm_ffn[l])
        h = h + (jax.nn.silu(hn @ w_ffn_gate[l]) * (hn @ w_ffn_up[l])) @ w_ffn_down[l]
    return rmsnorm(h, final_norm)[:, N_META:]
```

```python
import functools
import math

import jax
import jax.numpy as jnp
from jax import lax
from jax.experimental import pallas as pl
from jax.experimental.pallas import tpu as pltpu

F32 = jnp.float32
BF16 = jnp.bfloat16

CHUNK = 64
N_META = 16
NORM_EPS = 1e-6
NEG_INF = -1e30

CONV_WIDTH = 1024
CONV_KERNEL = 31
DIFF_HEADS = 8
DIFF_HEAD_DIM = 64
DIFF_V_DIM = 2 * DIFF_HEAD_DIM
REL_BUCKETS = 32
REL_MAX_DIST = 128
MLA_HEADS = 8
MLA_Q_RANK = 512
MLA_KV_RANK = 256
MLA_NOPE_DIM = 128
MLA_ROPE_DIM = 64
MLA_V_DIM = 128
ROPE_THETA = 10000.0

DIFF_QK = DIFF_HEADS * 2 * DIFF_HEAD_DIM
OFF_A = 0
OFF_BQ = OFF_A + 2 * CONV_WIDTH
OFF_BK = OFF_BQ + DIFF_QK
OFF_BV = OFF_BK + DIFF_QK
OFF_CQ = OFF_BV + DIFF_HEADS * DIFF_V_DIM
OFF_CKV = OFF_CQ + MLA_Q_RANK
OFF_CR = OFF_CKV + MLA_KV_RANK
OFF_G = OFF_CR + MLA_ROPE_DIM

LANE = 128
META_ROWS = 128
MLA_QK_PAD = 2 * LANE
MLA_UQ_PAD = 3 * LANE
VMEM_LIMIT = 56 * 1024 * 1024


def _params(*sem):
    return pltpu.CompilerParams(dimension_semantics=sem, vmem_limit_bytes=VMEM_LIMIT)


def _tile(n, cap, unit=LANE):
    best = None
    for t in range(unit, min(n, cap) + 1, unit):
        if n % t == 0:
            best = t
    assert best is not None, (n, cap, unit)
    return best


def _rms(x, g):
    ms = jnp.mean(x * x, axis=-1, keepdims=True)
    return x * lax.rsqrt(ms + NORM_EPS) * g


def _dot(a, b):
    return jnp.dot(a, b, preferred_element_type=F32)


def _dot_nt(a, b):
    return lax.dot_general(a, b, (((1,), (1,)), ((), ())), preferred_element_type=F32)


def _norm_mm_kernel(h_ref, g_ref, w_ref, o_ref, hn_ref):
    @pl.when(pl.program_id(1) == 0)
    def _():
        hn_ref[...] = _rms(h_ref[...], g_ref[...]).astype(BF16)

    o_ref[...] = _dot(hn_ref[...], w_ref[...]).astype(o_ref.dtype)


def _norm_matmul(h, g, w, out_dtype, tm, tn):
    t, d = h.shape
    n = w.shape[1]
    return pl.pallas_call(
        _norm_mm_kernel,
        out_shape=jax.ShapeDtypeStruct((t, n), out_dtype),
        grid=(t // tm, n // tn),
        in_specs=[
            pl.BlockSpec((tm, d), lambda i, j: (i, 0)),
            pl.BlockSpec((1, d), lambda i, j: (0, 0)),
            pl.BlockSpec((d, tn), lambda i, j: (0, j)),
        ],
        out_specs=pl.BlockSpec((tm, tn), lambda i, j: (i, j)),
        scratch_shapes=[pltpu.VMEM((tm, d), BF16)],
        compiler_params=_params("parallel", "arbitrary"),
        name="norm_qkv",
    )(h, g, w)


def _glu_kernel(h_ref, g_ref, wv_ref, wg_ref, o_ref, hn_ref):
    @pl.when(pl.program_id(1) == 0)
    def _():
        hn_ref[...] = _rms(h_ref[...], g_ref[...]).astype(BF16)

    hn = hn_ref[...]
    o_ref[...] = _dot(hn, wv_ref[...]) * jax.nn.sigmoid(_dot(hn, wg_ref[...]))


def _norm_glu(h, g, w_val, w_gate, tm, tn):
    t, d = h.shape
    n = w_val.shape[1]
    return pl.pallas_call(
        _glu_kernel,
        out_shape=jax.ShapeDtypeStruct((t, n), F32),
        grid=(t // tm, n // tn),
        in_specs=[
            pl.BlockSpec((tm, d), lambda i, j: (i, 0)),
            pl.BlockSpec((1, d), lambda i, j: (0, 0)),
            pl.BlockSpec((d, tn), lambda i, j: (0, j)),
            pl.BlockSpec((d, tn), lambda i, j: (0, j)),
        ],
        out_specs=pl.BlockSpec((tm, tn), lambda i, j: (i, j)),
        scratch_shapes=[pltpu.VMEM((tm, d), BF16)],
        compiler_params=_params("parallel", "arbitrary"),
        name="norm_glu",
    )(h, g, w_val, w_gate)


def _mla_proj_kernel(h_ref, g_ref, wc_ref, gq_ref, gkv_ref, wuq_ref, wukv_ref, cos_ref, sin_ref,
                     q_ref, k_ref, v_ref):
    hn = _rms(h_ref[...], g_ref[...]).astype(BF16)
    c = _dot(hn, wc_ref[...])
    cos = cos_ref[...]
    sin = sin_ref[...]
    o_kr = MLA_Q_RANK + MLA_KV_RANK
    kr = (c[:, o_kr:o_kr + LANE] * cos + c[:, o_kr + LANE:o_kr + 2 * LANE] * sin).astype(BF16)
    cqn = _rms(c[:, :MLA_Q_RANK], gq_ref[...]).astype(BF16)
    ckvn = _rms(c[:, MLA_Q_RANK:o_kr], gkv_ref[...]).astype(BF16)
    for hd in range(MLA_HEADS):
        q = _dot(cqn, wuq_ref[:, hd * MLA_UQ_PAD:(hd + 1) * MLA_UQ_PAD])
        kv = _dot(ckvn, wukv_ref[:, hd * 2 * LANE:(hd + 1) * 2 * LANE])
        qo = hd * MLA_QK_PAD
        q_ref[:, qo:qo + LANE] = q[:, :LANE].astype(BF16)
        q_ref[:, qo + LANE:qo + 2 * LANE] = (q[:, LANE:2 * LANE] * cos + q[:, 2 * LANE:] * sin).astype(BF16)
        k_ref[:, qo:qo + LANE] = kv[:, :LANE].astype(BF16)
        k_ref[:, qo + LANE:qo + 2 * LANE] = kr
        v_ref[:, hd * LANE:(hd + 1) * LANE] = kv[:, LANE:].astype(BF16)


def _mla_proj(h, g, wc, gq, gkv, wuq, wukv, cos_t, sin_t, tm):
    t, d = h.shape
    lp = cos_t.shape[0]
    nper = lp // tm
    const = lambda i: (0, 0)
    return pl.pallas_call(
        _mla_proj_kernel,
        out_shape=(
            jax.ShapeDtypeStruct((t, MLA_HEADS * MLA_QK_PAD), BF16),
            jax.ShapeDtypeStruct((t, MLA_HEADS * MLA_QK_PAD), BF16),
            jax.ShapeDtypeStruct((t, MLA_HEADS * MLA_V_DIM), BF16),
        ),
        grid=(t // tm,),
        in_specs=[
            pl.BlockSpec((tm, d), lambda i: (i, 0)),
            pl.BlockSpec((1, d), const),
            pl.BlockSpec(wc.shape, const),
            pl.BlockSpec(gq.shape, const),
            pl.BlockSpec(gkv.shape, const),
            pl.BlockSpec(wuq.shape, const),
            pl.BlockSpec(wukv.shape, const),
            pl.BlockSpec((tm, LANE), lambda i: (i % nper, 0)),
            pl.BlockSpec((tm, LANE), lambda i: (i % nper, 0)),
        ],
        out_specs=(
            pl.BlockSpec((tm, MLA_HEADS * MLA_QK_PAD), lambda i: (i, 0)),
            pl.BlockSpec((tm, MLA_HEADS * MLA_QK_PAD), lambda i: (i, 0)),
            pl.BlockSpec((tm, MLA_HEADS * MLA_V_DIM), lambda i: (i, 0)),
        ),
        compiler_params=_params("parallel"),
        name="mla_proj",
    )(h, g, wc, gq, gkv, wuq, wukv, cos_t, sin_t)


CONV_HALO = 32
CONV_ROWS = 32


def _conv_kernel(cur_ref, halo_ref, w_ref, b_ref, g_ref, beta_ref, o_ref, xs_ref, *, n_real_tiles):
    t = pl.program_id(1)
    tc = cur_ref.shape[1]
    c = cur_ref.shape[2]

    @pl.when(t == n_real_tiles)
    def _():
        xs_ref[0:CONV_HALO, :] = jnp.zeros((CONV_HALO, c), F32)

    @pl.when(t == 0)
    def _():
        xs_ref[0:CONV_HALO - N_META, :] = jnp.zeros((CONV_HALO - N_META, c), F32)
        xs_ref[CONV_HALO - N_META:CONV_HALO, :] = halo_ref[0, 0:N_META, :]

    @pl.when((t > 0) & (t < n_real_tiles))
    def _():
        xs_ref[0:CONV_HALO, :] = halo_ref[0]

    xs_ref[CONV_HALO:, :] = cur_ref[0]

    first_tap = CONV_HALO - (CONV_KERNEL - 1)
    for r in range(tc // CONV_ROWS):
        base = r * CONV_ROWS
        acc = jnp.broadcast_to(b_ref[...], (CONV_ROWS, c))
        for k in range(CONV_KERNEL):
            lo = base + first_tap + k
            acc = acc + w_ref[k:k + 1, :] * xs_ref[lo:lo + CONV_ROWS, :]
        mu = jnp.mean(acc, axis=-1, keepdims=True)
        ctr = acc - mu
        var = jnp.mean(ctr * ctr, axis=-1, keepdims=True)
        y = ctr * lax.rsqrt(var + NORM_EPS) * g_ref[...] + beta_ref[...]
        o_ref[0, base:base + CONV_ROWS, :] = (y * jax.nn.sigmoid(y)).astype(o_ref.dtype)


def _conv_branch(a, w, b, g, beta, seq):
    bsz, lp, c = a.shape
    tc = META_ROWS
    n_real = seq // tc
    hpt = tc // CONV_HALO

    def halo_map(bi, t):
        inside = (t > 0) & (t < n_real)
        return (bi, jnp.where(inside, t * hpt - 1, seq // CONV_HALO), 0)

    const = lambda bi, t: (0, 0)
    return pl.pallas_call(
        functools.partial(_conv_kernel, n_real_tiles=n_real),
        out_shape=jax.ShapeDtypeStruct((bsz, lp, c), BF16),
        grid=(bsz, n_real + 1),
        in_specs=[
            pl.BlockSpec((1, tc, c), lambda bi, t: (bi, t, 0)),
            pl.BlockSpec((1, CONV_HALO, c), halo_map),
            pl.BlockSpec(w.shape, const),
            pl.BlockSpec((1, c), const),
            pl.BlockSpec((1, c), const),
            pl.BlockSpec((1, c), const),
        ],
        out_specs=pl.BlockSpec((1, tc, c), lambda bi, t: (bi, t, 0)),
        scratch_shapes=[pltpu.VMEM((CONV_HALO + tc, c), F32)],
        compiler_params=_params("parallel", "arbitrary"),
        name="conv_ln_swish",
    )(a, a, w, b, g, beta)


def _softmax_step(s, v, m_ref, l_ref, acc_ref, i):
    m_prev = m_ref[i]
    m_new = jnp.maximum(m_prev, jnp.max(s, axis=1, keepdims=True))
    alpha = jnp.exp(m_prev - m_new)
    p = jnp.exp(s - m_new)
    l_ref[i] = alpha * l_ref[i] + jnp.sum(p, axis=1, keepdims=True)
    acc_ref[i] = alpha * acc_ref[i] + _dot(p.astype(BF16), v)
    m_ref[i] = m_new


def _init_state(m_ref, l_ref, acc_ref):
    m_ref[...] = jnp.full(m_ref.shape, NEG_INF, F32)
    l_ref[...] = jnp.zeros(l_ref.shape, F32)
    acc_ref[...] = jnp.zeros(acc_ref.shape, F32)


def _split_diff_q(q):
    lane = lax.broadcasted_iota(jnp.int32, q.shape, 1)
    qs = q * jnp.asarray(DIFF_HEAD_DIM ** -0.5, q.dtype)
    zero = jnp.zeros_like(qs)
    return jnp.where(lane < DIFF_HEAD_DIM, qs, zero), jnp.where(lane < DIFF_HEAD_DIM, zero, qs)


def _diff_lambda(lq1, lk1, lq2, lk2, lam_init):
    return (jnp.exp(jnp.sum(lq1 * lk1, axis=-1, keepdims=True))
            - jnp.exp(jnp.sum(lq2 * lk2, axis=-1, keepdims=True)) + lam_init)


def _diff_finish(o1, o2, lam, subln, lam_init):
    o = o1 - lam * o2
    return _rms(o, subln) * (1.0 - lam_init)


def _diff_kernel(q_ref, k_ref, v_ref, km_ref, vm_ref, bd_ref, bs_ref, bm_ref,
                 lq1_ref, lk1_ref, lq2_ref, lk2_ref, sub_ref, li_ref,
                 o_ref, m_ref, l_ref, acc_ref):
    qi = pl.program_id(2)
    tq = q_ref.shape[1]
    q1, q2 = _split_diff_q(q_ref[0])
    _init_state(m_ref, l_ref, acc_ref)

    def step(k, v, bias):
        s1 = _dot_nt(q1, k)
        s2 = _dot_nt(q2, k)
        if bias is not None:
            s1 = s1 + bias
            s2 = s2 + bias
        _softmax_step(s1, v, m_ref, l_ref, acc_ref, 0)
        _softmax_step(s2, v, m_ref, l_ref, acc_ref, 1)

    step(km_ref[0], vm_ref[0], bm_ref[0, 0])

    def far(j, carry):
        start = pl.multiple_of(j * tq, tq)
        step(k_ref[0, pl.ds(start, tq), :], v_ref[0, pl.ds(start, tq), :], None)
        return carry

    lax.fori_loop(0, jnp.maximum(qi - 1, 0), far, 0)

    @pl.when(qi > 0)
    def _():
        start = pl.multiple_of((qi - 1) * tq, tq)
        step(k_ref[0, pl.ds(start, tq), :], v_ref[0, pl.ds(start, tq), :], bs_ref[0])

    start = pl.multiple_of(qi * tq, tq)
    step(k_ref[0, pl.ds(start, tq), :], v_ref[0, pl.ds(start, tq), :], bd_ref[0])

    li = li_ref[...]
    lam = _diff_lambda(lq1_ref[...], lk1_ref[...], lq2_ref[...], lk2_ref[...], li)
    o = _diff_finish(acc_ref[0] / l_ref[0], acc_ref[1] / l_ref[1], lam, sub_ref[...], li)
    o_ref[0] = o.astype(o_ref.dtype)


def _diff_meta_kernel(q_ref, k_ref, v_ref, bm_ref, lq1_ref, lk1_ref, lq2_ref, lk2_ref, sub_ref, li_ref,
                      prev_ref, o_ref):
    del prev_ref
    li = li_ref[...]
    lam = _diff_lambda(lq1_ref[...], lk1_ref[...], lq2_ref[...], lk2_ref[...], li)
    for hd in range(DIFF_HEADS):
        sl = slice(hd * LANE, (hd + 1) * LANE)
        q1, q2 = _split_diff_q(q_ref[0, :, sl])
        k = k_ref[0, :, sl]
        v = v_ref[0, :, sl]
        outs = []
        for qz in (q1, q2):
            s = _dot_nt(qz, k) + bm_ref[hd]
            p = jnp.exp(s - jnp.max(s, axis=1, keepdims=True))
            outs.append(_dot(p.astype(BF16), v) / jnp.sum(p, axis=1, keepdims=True))
        o_ref[0, :, sl] = _diff_finish(outs[0], outs[1], lam, sub_ref[...], li).astype(o_ref.dtype)


def _diff_attention(qkv, bias, lq1, lk1, lq2, lk2, subln, lam_init, seq, tq):
    bsz, lp, _ = qkv.shape
    nq = seq // tq
    mb = seq // META_ROWS
    hq = DIFF_HEADS
    vec = lambda n: pl.BlockSpec((1, n), lambda *_: (0, 0))
    small = [vec(DIFF_HEAD_DIM)] * 4 + [vec(DIFF_V_DIM), vec(1)]
    smalls = (lq1, lk1, lq2, lk2, subln, lam_init)
    out = pl.pallas_call(
        _diff_kernel,
        out_shape=jax.ShapeDtypeStruct((bsz, lp, hq * DIFF_V_DIM), BF16),
        grid=(bsz, hq, nq),
        in_specs=[
            pl.BlockSpec((1, tq, LANE), lambda b, h, i: (b, i, h)),
            pl.BlockSpec((1, seq, LANE), lambda b, h, i: (b, 0, hq + h)),
            pl.BlockSpec((1, seq, LANE), lambda b, h, i: (b, 0, 2 * hq + h)),
            pl.BlockSpec((1, META_ROWS, LANE), lambda b, h, i: (b, mb, hq + h)),
            pl.BlockSpec((1, META_ROWS, LANE), lambda b, h, i: (b, mb, 2 * hq + h)),
            pl.BlockSpec((1, tq, tq), lambda b, h, i: (h, 0, 0)),
            pl.BlockSpec((1, tq, tq), lambda b, h, i: (h, 0, 0)),
            pl.BlockSpec((1, 1, tq, META_ROWS), lambda b, h, i: (h, jnp.minimum(i, 1), 0, 0)),
        ] + small,
        out_specs=pl.BlockSpec((1, tq, LANE), lambda b, h, i: (b, i, h)),
        scratch_shapes=[pltpu.VMEM((2, tq, 1), F32), pltpu.VMEM((2, tq, 1), F32),
                        pltpu.VMEM((2, tq, DIFF_V_DIM), F32)],
        compiler_params=_params("parallel", "parallel", "arbitrary"),
        name="diff_attention",
    )(qkv, qkv, qkv, qkv, qkv, bias["diag"], bias["sub"], bias["meta_keys"], *smalls)

    wide = hq * LANE
    return pl.pallas_call(
        _diff_meta_kernel,
        out_shape=jax.ShapeDtypeStruct(out.shape, out.dtype),
        grid=(bsz,),
        in_specs=[
            pl.BlockSpec((1, META_ROWS, wide), lambda b: (b, mb, 0)),
            pl.BlockSpec((1, META_ROWS, wide), lambda b: (b, mb, 1)),
            pl.BlockSpec((1, META_ROWS, wide), lambda b: (b, mb, 2)),
            pl.BlockSpec(bias["meta_meta"].shape, lambda b: (0, 0, 0)),
        ] + small + [pl.BlockSpec(memory_space=pl.ANY)],
        out_specs=pl.BlockSpec((1, META_ROWS, wide), lambda b: (b, mb, 0)),
        input_output_aliases={10: 0},
        compiler_params=_params("parallel"),
        name="diff_attention_meta",
    )(qkv, qkv, qkv, bias["meta_meta"], *smalls, out)


def _mla_kernel(q_ref, k_ref, v_ref, km_ref, vm_ref, md_ref, mm_ref, o_ref, m_ref, l_ref, acc_ref, *, scale):
    qi = pl.program_id(2)
    tq = q_ref.shape[1]
    q = q_ref[0]
    _init_state(m_ref, l_ref, acc_ref)

    def step(k, v, mask):
        s = _dot_nt(q, k) * scale
        if mask is not None:
            s = s + mask
        _softmax_step(s, v, m_ref, l_ref, acc_ref, 0)

    step(km_ref[0], vm_ref[0], mm_ref[...])

    def far(j, carry):
        start = pl.multiple_of(j * tq, tq)
        step(k_ref[0, pl.ds(start, tq), :], v_ref[0, pl.ds(start, tq), :], None)
        return carry

    lax.fori_loop(0, qi, far, 0)

    start = pl.multiple_of(qi * tq, tq)
    step(k_ref[0, pl.ds(start, tq), :], v_ref[0, pl.ds(start, tq), :], md_ref[...])
    o_ref[0] = (acc_ref[0] / l_ref[0]).astype(o_ref.dtype)


def _mla_meta_kernel(q_ref, k_ref, v_ref, mm_ref, prev_ref, o_ref, *, scale):
    del prev_ref
    for hd in range(MLA_HEADS):
        qk = slice(hd * MLA_QK_PAD, (hd + 1) * MLA_QK_PAD)
        vs = slice(hd * MLA_V_DIM, (hd + 1) * MLA_V_DIM)
        s = _dot_nt(q_ref[0, :, qk], k_ref[0, :, qk]) * scale + mm_ref[...]
        p = jnp.exp(s - jnp.max(s, axis=1, keepdims=True))
        o = _dot(p.astype(BF16), v_ref[0, :, vs]) / jnp.sum(p, axis=1, keepdims=True)
        o_ref[0, :, vs] = o.astype(o_ref.dtype)


def _mla_attention(qm, km, vm, masks, seq, tq):
    bsz, lp, _ = qm.shape
    nq = seq // tq
    mb = seq // META_ROWS
    scale = (MLA_NOPE_DIM + MLA_ROPE_DIM) ** -0.5
    out = pl.pallas_call(
        functools.partial(_mla_kernel, scale=scale),
        out_shape=jax.ShapeDtypeStruct((bsz, lp, MLA_HEADS * MLA_V_DIM), BF16),
        grid=(bsz, MLA_HEADS, nq),
        in_specs=[
            pl.BlockSpec((1, tq, MLA_QK_PAD), lambda b, h, i: (b, i, h)),
            pl.BlockSpec((1, seq, MLA_QK_PAD), lambda b, h, i: (b, 0, h)),
            pl.BlockSpec((1, seq, MLA_V_DIM), lambda b, h, i: (b, 0, h)),
            pl.BlockSpec((1, META_ROWS, MLA_QK_PAD), lambda b, h, i: (b, mb, h)),
            pl.BlockSpec((1, META_ROWS, MLA_V_DIM), lambda b, h, i: (b, mb, h)),
            pl.BlockSpec((tq, tq), lambda b, h, i: (0, 0)),
            pl.BlockSpec((1, META_ROWS), lambda b, h, i: (0, 0)),
        ],
        out_specs=pl.BlockSpec((1, tq, MLA_V_DIM), lambda b, h, i: (b, i, h)),
        scratch_shapes=[pltpu.VMEM((1, tq, 1), F32), pltpu.VMEM((1, tq, 1), F32),
                        pltpu.VMEM((1, tq, MLA_V_DIM), F32)],
        compiler_params=_params("parallel", "parallel", "arbitrary"),
        name="mla_attention",
    )(qm, km, vm, km, vm, masks["diag"], masks["meta_cols"])

    return pl.pallas_call(
        functools.partial(_mla_meta_kernel, scale=scale),
        out_shape=jax.ShapeDtypeStruct(out.shape, out.dtype),
        grid=(bsz,),
        in_specs=[
            pl.BlockSpec((1, META_ROWS, MLA_HEADS * MLA_QK_PAD), lambda b: (b, mb, 0)),
            pl.BlockSpec((1, META_ROWS, MLA_HEADS * MLA_QK_PAD), lambda b: (b, mb, 0)),
            pl.BlockSpec((1, META_ROWS, MLA_HEADS * MLA_V_DIM), lambda b: (b, mb, 0)),
            pl.BlockSpec((1, META_ROWS), lambda b: (0, 0)),
            pl.BlockSpec(memory_space=pl.ANY),
        ],
        out_specs=pl.BlockSpec((1, META_ROWS, MLA_HEADS * MLA_V_DIM), lambda b: (b, mb, 0)),
        input_output_aliases={4: 0},
        compiler_params=_params("parallel"),
        name="mla_attention_meta",
    )(qm, km, vm, masks["meta_cols"], out)


def _mix_kernel(h_ref, g_ref, xa_ref, xb_ref, xc_ref, ga_ref, gb_ref, gc_ref, wa_ref, wb_ref, wc_ref,
                o_ref, hn_ref):
    @pl.when(pl.program_id(1) == 0)
    def _():
        hn_ref[...] = _rms(h_ref[...], g_ref[...]).astype(BF16)

    hn = hn_ref[...]
    mixed = jax.nn.sigmoid(_dot(hn, ga_ref[...])) * _dot(xa_ref[...], wa_ref[...])
    mixed = mixed + jax.nn.sigmoid(_dot(hn, gb_ref[...])) * _dot(xb_ref[...], wb_ref[...])
    mixed = mixed + jax.nn.sigmoid(_dot(hn, gc_ref[...])) * _dot(xc_ref[...], wc_ref[...])
    o_ref[...] = mixed.astype(o_ref.dtype)


def _gated_mix(h, g, xa, xb, xc, w_gates, wa, wb, wc, tm, tn):
    t, d = h.shape
    nj = d // tn
    branch = lambda x: pl.BlockSpec((tm, x.shape[1]), lambda i, j: (i, 0))
    gate = lambda n: pl.BlockSpec((d, tn), lambda i, j: (0, n * nj + j))
    wbr = lambda w: pl.BlockSpec((w.shape[0], tn), lambda i, j: (0, j))
    return pl.pallas_call(
        _mix_kernel,
        out_shape=jax.ShapeDtypeStruct((t, d), BF16),
        grid=(t // tm, nj),
        in_specs=[
            pl.BlockSpec((tm, d), lambda i, j: (i, 0)),
            pl.BlockSpec((1, d), lambda i, j: (0, 0)),
            branch(xa), branch(xb), branch(xc),
            gate(0), gate(1), gate(2),
            wbr(wa), wbr(wb), wbr(wc),
        ],
        out_specs=pl.BlockSpec((tm, tn), lambda i, j: (i, j)),
        scratch_shapes=[pltpu.VMEM((tm, d), BF16)],
        compiler_params=_params("parallel", "arbitrary"),
        name="gated_mix",
    )(h, g, xa, xb, xc, w_gates, w_gates, w_gates, wa, wb, wc)


def _residual_mm_kernel(h_ref, x_ref, w_ref, o_ref):
    o_ref[...] = h_ref[...] + _dot(x_ref[...], w_ref[...])


def _residual_matmul(h, x, w, tm, tn):
    t, d = h.shape
    k = x.shape[1]
    return pl.pallas_call(
        _residual_mm_kernel,
        out_shape=jax.ShapeDtypeStruct((t, d), F32),
        grid=(t // tm, d // tn),
        in_specs=[
            pl.BlockSpec((tm, tn), lambda i, j: (i, j)),
            pl.BlockSpec((tm, k), lambda i, j: (i, 0)),
            pl.BlockSpec((k, tn), lambda i, j: (0, j)),
        ],
        out_specs=pl.BlockSpec((tm, tn), lambda i, j: (i, j)),
        compiler_params=_params("parallel", "arbitrary"),
        name="out_proj",
    )(h, x, w)


def _ffn_kernel(h_ref, g_ref, wg_ref, wu_ref, wd_ref, o_ref, hn_ref):
    f = pl.program_id(1)

    @pl.when(f == 0)
    def _():
        hn_ref[...] = _rms(h_ref[...], g_ref[...]).astype(BF16)

    hn = hn_ref[...]
    gate = _dot(hn, wg_ref[...])
    act = (gate * jax.nn.sigmoid(gate) * _dot(hn, wu_ref[...])).astype(BF16)
    part = _dot(act, wd_ref[...])

    @pl.when(f == 0)
    def _():
        o_ref[...] = h_ref[...] + part

    @pl.when(f > 0)
    def _():
        o_ref[...] += part


def _ffn(h, g, wg, wu, wd, tm, tf):
    t, d = h.shape
    dff = wg.shape[1]
    return pl.pallas_call(
        _ffn_kernel,
        out_shape=jax.ShapeDtypeStruct((t, d), F32),
        grid=(t // tm, dff // tf),
        in_specs=[
            pl.BlockSpec((tm, d), lambda i, f: (i, 0)),
            pl.BlockSpec((1, d), lambda i, f: (0, 0)),
            pl.BlockSpec((d, tf), lambda i, f: (0, f)),
            pl.BlockSpec((d, tf), lambda i, f: (0, f)),
            pl.BlockSpec((tf, d), lambda i, f: (f, 0)),
        ],
        out_specs=pl.BlockSpec((tm, d), lambda i, f: (i, 0)),
        scratch_shapes=[pltpu.VMEM((tm, d), BF16)],
        compiler_params=_params("parallel", "arbitrary"),
        name="swiglu_ffn",
    )(h, g, wg, wu, wd)


def _final_norm_kernel(h_ref, g_ref, o_ref):
    o_ref[0] = _rms(h_ref[0], g_ref[...])


def _final_norm(h3, g, seq, tr):
    bsz, _, d = h3.shape
    return pl.pallas_call(
        _final_norm_kernel,
        out_shape=jax.ShapeDtypeStruct((bsz, seq, d), F32),
        grid=(bsz, seq // tr),
        in_specs=[
            pl.BlockSpec((1, tr, d), lambda b, i: (b, i, 0)),
            pl.BlockSpec((1, d), lambda b, i: (0, 0)),
        ],
        out_specs=pl.BlockSpec((1, tr, d), lambda b, i: (b, i, 0)),
        compiler_params=_params("parallel", "parallel"),
        name="final_norm",
    )(h3, g)


def _t5_bucket(rel):
    nb = REL_BUCKETS // 2
    ret = jnp.where(rel > 0, nb, 0)
    n = jnp.abs(rel)
    max_exact = nb // 2
    nf = jnp.maximum(n, 1).astype(F32)
    large = max_exact + (jnp.log(nf / max_exact) / math.log(REL_MAX_DIST / max_exact)
                         * (nb - max_exact)).astype(jnp.int32)
    large = jnp.minimum(large, nb - 1)
    return ret + jnp.where(n < max_exact, n, large)


def _position_tables(rel_bias, seq, tq):
    far = rel_bias[_t5_bucket(jnp.asarray(-REL_MAX_DIST, jnp.int32))]

    def bias(rel, visible):
        t = jnp.transpose(rel_bias[_t5_bucket(rel)] - far, (2, 0, 1)).astype(F32)
        return jnp.where(visible[None], t, NEG_INF)

    r = jnp.arange(tq, dtype=jnp.int32)[:, None]
    c = jnp.arange(tq, dtype=jnp.int32)[None, :]
    chunk_vis = (c // CHUNK) <= (r // CHUNK)
    m = jnp.arange(META_ROWS, dtype=jnp.int32)
    meta_vis = jnp.broadcast_to((m < N_META)[None, :], (tq, META_ROWS))
    diff = {
        "diag": bias(c - r, chunk_vis),
        "sub": bias(c - tq - r, jnp.ones((tq, tq), bool)),
        "meta_keys": jnp.stack([bias(m[None, :] - N_META - r, meta_vis),
                                bias(jnp.full((tq, META_ROWS), -REL_MAX_DIST - tq, jnp.int32), meta_vis)],
                               axis=1),
        "meta_meta": bias(m[None, :] - m[:, None],
                          jnp.broadcast_to((m < N_META)[None, :], (META_ROWS, META_ROWS))),
    }
    mla = {
        "diag": jnp.where(chunk_vis, 0.0, NEG_INF).astype(F32),
        "meta_cols": jnp.where(m < N_META, 0.0, NEG_INF).astype(F32)[None, :],
    }
    return diff, mla


def _rope_tables(seq):
    half = MLA_ROPE_DIM // 2
    row = jnp.arange(seq + META_ROWS, dtype=jnp.int32)
    pos = jnp.where(row < seq, row + N_META, row - seq)
    inv = ROPE_THETA ** (-jnp.arange(half, dtype=F32) / half)
    ang = pos.astype(F32)[:, None] * inv[None, :]
    cos, sin = jnp.cos(ang), jnp.sin(ang)
    pad = jnp.zeros((seq + META_ROWS, LANE - MLA_ROPE_DIM), F32)
    return (jnp.concatenate([cos, cos, pad], axis=1), jnp.concatenate([-sin, sin, pad], axis=1))


def _swap_halves(w):
    half = w.shape[-1] // 2
    return jnp.concatenate([w[..., half:], w[..., :half]], axis=-1)


def _lane_pad(w):
    return jnp.concatenate([w, jnp.zeros(w.shape[:-1] + (LANE - w.shape[-1],), w.dtype)], axis=-1)


def kernel(x, meta_tokens, rel_bias, norm_mix, w_in, conv_w, conv_b, conv_ln_g, conv_ln_b, w_br_conv,
           diff_lq1, diff_lk1, diff_lq2, diff_lk2, diff_subln, w_br_diff, mla_q_norm, w_uq, mla_kv_norm,
           w_ukv, w_br_mla, w_out, norm_ffn, w_ffn_gate, w_ffn_up, w_ffn_down, final_norm):
    bsz, seq, d = x.shape
    depth = w_in.shape[0]
    dff = w_ffn_gate.shape[-1]
    lp = seq + META_ROWS
    t = bsz * lp
    assert seq % META_ROWS == 0 and d % LANE == 0

    tq = _tile(seq, 512)
    tm = _tile(lp, 640)
    tn = _tile(d, 512)
    tf = _tile(dff, 512)
    tn_mix = _tile(d, 256)

    w_in_b = w_in.astype(BF16)
    w_val = w_in_b[:, :, OFF_A:OFF_A + CONV_WIDTH]
    w_gate = w_in_b[:, :, OFF_A + CONV_WIDTH:OFF_BQ]
    w_qkv = w_in_b[:, :, OFF_BQ:OFF_CQ]
    w_kr = w_in_b[:, :, OFF_CR:OFF_G]
    w_c = jnp.concatenate([w_in_b[:, :, OFF_CQ:OFF_CR], _lane_pad(w_kr), _lane_pad(_swap_halves(w_kr))], axis=-1)
    w_gates = w_in_b[:, :, OFF_G:]
    uq = w_uq.astype(BF16).reshape(depth, MLA_Q_RANK, MLA_HEADS, MLA_NOPE_DIM + MLA_ROPE_DIM)
    uq_rope = uq[..., MLA_NOPE_DIM:]
    w_uq_p = jnp.concatenate([uq[..., :MLA_NOPE_DIM], _lane_pad(uq_rope), _lane_pad(_swap_halves(uq_rope))],
                             axis=-1).reshape(depth, MLA_Q_RANK, MLA_HEADS * MLA_UQ_PAD)
    w_ukv_b = w_ukv.astype(BF16)
    w_bra, w_brb, w_brc = w_br_conv.astype(BF16), w_br_diff.astype(BF16), w_br_mla.astype(BF16)
    w_out_b = w_out.astype(BF16)
    w_fg, w_fu, w_fd = w_ffn_gate.astype(BF16), w_ffn_up.astype(BF16), w_ffn_down.astype(BF16)

    diff_tabs, mla_tabs = _position_tables(rel_bias, seq, tq)
    cos_t, sin_t = _rope_tables(seq)

    row = lambda v: v.reshape(1, -1)
    h = jnp.concatenate([x, jnp.broadcast_to(meta_tokens[None].astype(x.dtype), (bsz, N_META, d)),
                         jnp.zeros((bsz, META_ROWS - N_META, d), x.dtype)], axis=1).reshape(t, d)

    for l in range(depth):
        lam_init = jnp.full((1, 1), 0.8 - 0.6 * math.exp(-0.3 * l), F32)
        g_mix = row(norm_mix[l])
        a = _norm_glu(h, g_mix, w_val[l], w_gate[l], tm, tn)
        qkv = _norm_matmul(h, g_mix, w_qkv[l], BF16, tm, tn)
        qm, km, vm = _mla_proj(h, g_mix, w_c[l], row(mla_q_norm[l]), row(mla_kv_norm[l]), w_uq_p[l],
                               w_ukv_b[l], cos_t, sin_t, tm)
        ya = _conv_branch(a.reshape(bsz, lp, CONV_WIDTH), conv_w[l], row(conv_b[l]), row(conv_ln_g[l]),
                          row(conv_ln_b[l]), seq)
        yb = _diff_attention(qkv.reshape(bsz, lp, -1), diff_tabs, row(diff_lq1[l]), row(diff_lk1[l]),
                             row(diff_lq2[l]), row(diff_lk2[l]), row(diff_subln[l]), lam_init, seq, tq)
        yc = _mla_attention(qm.reshape(bsz, lp, -1), km.reshape(bsz, lp, -1), vm.reshape(bsz, lp, -1),
                            mla_tabs, seq, tq)
        mixed = _gated_mix(h, g_mix, ya.reshape(t, -1), yb.reshape(t, -1), yc.reshape(t, -1),
                           w_gates[l], w_bra[l], w_brb[l], w_brc[l], tm, tn_mix)
        h = _residual_matmul(h, mixed, w_out_b[l], tm, tn)
        h = _ffn(h, row(norm_ffn[l]), w_fg[l], w_fu[l], w_fd[l], tm, tf)

    return _final_norm(h.reshape(bsz, lp, d), row(final_norm), seq, _tile(seq, 512))
```

```python
import functools
import math

import jax
import jax.numpy as jnp
from jax import lax
from jax.experimental import pallas as pl
from jax.experimental.pallas import tpu as pltpu

F32 = jnp.float32
BF16 = jnp.bfloat16

CHUNK = 64
N_META = 16
NORM_EPS = 1e-6
NEG_INF = -1e30

CONV_WIDTH = 1024
CONV_KERNEL = 31
DIFF_HEADS = 8
DIFF_HEAD_DIM = 64
DIFF_V_DIM = 2 * DIFF_HEAD_DIM
REL_BUCKETS = 32
REL_MAX_DIST = 128
MLA_HEADS = 8
MLA_Q_RANK = 512
MLA_KV_RANK = 256
MLA_NOPE_DIM = 128
MLA_ROPE_DIM = 64
MLA_V_DIM = 128
ROPE_THETA = 10000.0

DIFF_QK = DIFF_HEADS * 2 * DIFF_HEAD_DIM
OFF_A = 0
OFF_BQ = OFF_A + 2 * CONV_WIDTH
OFF_BK = OFF_BQ + DIFF_QK
OFF_BV = OFF_BK + DIFF_QK
OFF_CQ = OFF_BV + DIFF_HEADS * DIFF_V_DIM
OFF_CKV = OFF_CQ + MLA_Q_RANK
OFF_CR = OFF_CKV + MLA_KV_RANK
OFF_G = OFF_CR + MLA_ROPE_DIM

LANE = 128
META_ROWS = 128
MLA_QK_PAD = 2 * LANE
MLA_UQ_PAD = 3 * LANE
VMEM_LIMIT = 56 * 1024 * 1024


def _params(*sem):
    return pltpu.CompilerParams(dimension_semantics=sem, vmem_limit_bytes=VMEM_LIMIT)


def _tile(n, cap, unit=LANE):
    best = None
    for t in range(unit, min(n, cap) + 1, unit):
        if n % t == 0:
            best = t
    assert best is not None, (n, cap, unit)
    return best


def _rms(x, g):
    ms = jnp.mean(x * x, axis=-1, keepdims=True)
    return x * lax.rsqrt(ms + NORM_EPS) * g


def _dot(a, b):
    return jnp.dot(a, b, preferred_element_type=F32)


def _dot_nt(a, b):
    return lax.dot_general(a, b, (((1,), (1,)), ((), ())), preferred_element_type=F32)


def _norm_mm_kernel(h_ref, g_ref, w_ref, o_ref, hn_ref):
    @pl.when(pl.program_id(1) == 0)
    def _():
        hn_ref[...] = _rms(h_ref[...], g_ref[...]).astype(BF16)

    o_ref[...] = _dot(hn_ref[...], w_ref[...]).astype(o_ref.dtype)


def _norm_matmul(h, g, w, out_dtype, tm, tn):
    t, d = h.shape
    n = w.shape[1]
    return pl.pallas_call(
        _norm_mm_kernel,
        out_shape=jax.ShapeDtypeStruct((t, n), out_dtype),
        grid=(t // tm, n // tn),
        in_specs=[
            pl.BlockSpec((tm, d), lambda i, j: (i, 0)),
            pl.BlockSpec((1, d), lambda i, j: (0, 0)),
            pl.BlockSpec((d, tn), lambda i, j: (0, j)),
        ],
        out_specs=pl.BlockSpec((tm, tn), lambda i, j: (i, j)),
        scratch_shapes=[pltpu.VMEM((tm, d), BF16)],
        compiler_params=_params("parallel", "arbitrary"),
        name="norm_qkv",
    )(h, g, w)


def _glu_kernel(h_ref, g_ref, wv_ref, wg_ref, o_ref, hn_ref):
    @pl.when(pl.program_id(1) == 0)
    def _():
        hn_ref[...] = _rms(h_ref[...], g_ref[...]).astype(BF16)

    hn = hn_ref[...]
    o_ref[...] = _dot(hn, wv_ref[...]) * jax.nn.sigmoid(_dot(hn, wg_ref[...]))


def _norm_glu(h, g, w_val, w_gate, tm, tn):
    t, d = h.shape
    n = w_val.shape[1]
    return pl.pallas_call(
        _glu_kernel,
        out_shape=jax.ShapeDtypeStruct((t, n), F32),
        grid=(t // tm, n // tn),
        in_specs=[
            pl.BlockSpec((tm, d), lambda i, j: (i, 0)),
            pl.BlockSpec((1, d), lambda i, j: (0, 0)),
            pl.BlockSpec((d, tn), lambda i, j: (0, j)),
            pl.BlockSpec((d, tn), lambda i, j: (0, j)),
        ],
        out_specs=pl.BlockSpec((tm, tn), lambda i, j: (i, j)),
        scratch_shapes=[pltpu.VMEM((tm, d), BF16)],
        compiler_params=_params("parallel", "arbitrary"),
        name="norm_glu",
    )(h, g, w_val, w_gate)


def _mla_proj_kernel(h_ref, g_ref, wc_ref, gq_ref, gkv_ref, wuq_ref, wukv_ref, cos_ref, sin_ref,
                     q_ref, k_ref, v_ref):
    hn = _rms(h_ref[...], g_ref[...]).astype(BF16)
    c = _dot(hn, wc_ref[...])
    cos = cos_ref[...]
    sin = sin_ref[...]
    o_kr = MLA_Q_RANK + MLA_KV_RANK
    kr = (c[:, o_kr:o_kr + LANE] * cos + c[:, o_kr + LANE:o_kr + 2 * LANE] * sin).astype(BF16)
    cqn = _rms(c[:, :MLA_Q_RANK], gq_ref[...]).astype(BF16)
    ckvn = _rms(c[:, MLA_Q_RANK:o_kr], gkv_ref[...]).astype(BF16)
    for hd in range(MLA_HEADS):
        q = _dot(cqn, wuq_ref[:, hd * MLA_UQ_PAD:(hd + 1) * MLA_UQ_PAD])
        kv = _dot(ckvn, wukv_ref[:, hd * 2 * LANE:(hd + 1) * 2 * LANE])
        qo = hd * MLA_QK_PAD
        q_ref[:, qo:qo + LANE] = q[:, :LANE].astype(BF16)
        q_ref[:, qo + LANE:qo + 2 * LANE] = (q[:, LANE:2 * LANE] * cos + q[:, 2 * LANE:] * sin).astype(BF16)
        k_ref[:, qo:qo + LANE] = kv[:, :LANE].astype(BF16)
        k_ref[:, qo + LANE:qo + 2 * LANE] = kr
        v_ref[:, hd * LANE:(hd + 1) * LANE] = kv[:, LANE:].astype(BF16)


def _mla_proj(h, g, wc, gq, gkv, wuq, wukv, cos_t, sin_t, tm):
    t, d = h.shape
    lp = cos_t.shape[0]
    nper = lp // tm
    const = lambda i: (0, 0)
    return pl.pallas_call(
        _mla_proj_kernel,
        out_shape=(
            jax.ShapeDtypeStruct((t, MLA_HEADS * MLA_QK_PAD), BF16),
            jax.ShapeDtypeStruct((t, MLA_HEADS * MLA_QK_PAD), BF16),
            jax.ShapeDtypeStruct((t, MLA_HEADS * MLA_V_DIM), BF16),
        ),
        grid=(t // tm,),
        in_specs=[
            pl.BlockSpec((tm, d), lambda i: (i, 0)),
            pl.BlockSpec((1, d), const),
            pl.BlockSpec(wc.shape, const),
            pl.BlockSpec(gq.shape, const),
            pl.BlockSpec(gkv.shape, const),
            pl.BlockSpec(wuq.shape, const),
            pl.BlockSpec(wukv.shape, const),
            pl.BlockSpec((tm, LANE), lambda i: (i % nper, 0)),
            pl.BlockSpec((tm, LANE), lambda i: (i % nper, 0)),
        ],
        out_specs=(
            pl.BlockSpec((tm, MLA_HEADS * MLA_QK_PAD), lambda i: (i, 0)),
            pl.BlockSpec((tm, MLA_HEADS * MLA_QK_PAD), lambda i: (i, 0)),
            pl.BlockSpec((tm, MLA_HEADS * MLA_V_DIM), lambda i: (i, 0)),
        ),
        compiler_params=_params("parallel"),
        name="mla_proj",
    )(h, g, wc, gq, gkv, wuq, wukv, cos_t, sin_t)


CONV_HALO = 32
CONV_ROWS = 32


def _conv_kernel(cur_ref, halo_ref, w_ref, b_ref, g_ref, beta_ref, o_ref, xs_ref, *, n_real_tiles):
    t = pl.program_id(1)
    tc = cur_ref.shape[1]
    c = cur_ref.shape[2]

    @pl.when(t == n_real_tiles)
    def _():
        xs_ref[0:CONV_HALO, :] = jnp.zeros((CONV_HALO, c), F32)

    @pl.when(t == 0)
    def _():
        xs_ref[0:CONV_HALO - N_META, :] = jnp.zeros((CONV_HALO - N_META, c), F32)
        xs_ref[CONV_HALO - N_META:CONV_HALO, :] = halo_ref[0, 0:N_META, :]

    @pl.when((t > 0) & (t < n_real_tiles))
    def _():
        xs_ref[0:CONV_HALO, :] = halo_ref[0]

    xs_ref[CONV_HALO:, :] = cur_ref[0]

    first_tap = CONV_HALO - (CONV_KERNEL - 1)
    for r in range(tc // CONV_ROWS):
        base = r * CONV_ROWS
        acc = jnp.broadcast_to(b_ref[...], (CONV_ROWS, c))
        for k in range(CONV_KERNEL):
            lo = base + first_tap + k
            acc = acc + w_ref[k:k + 1, :] * xs_ref[lo:lo + CONV_ROWS, :]
        mu = jnp.mean(acc, axis=-1, keepdims=True)
        ctr = acc - mu
        var = jnp.mean(ctr * ctr, axis=-1, keepdims=True)
        y = ctr * lax.rsqrt(var + NORM_EPS) * g_ref[...] + beta_ref[...]
        o_ref[0, base:base + CONV_ROWS, :] = (y * jax.nn.sigmoid(y)).astype(o_ref.dtype)


def _conv_branch(a, w, b, g, beta, seq):
    bsz, lp, c = a.shape
    tc = META_ROWS
    n_real = seq // tc
    hpt = tc // CONV_HALO

    def halo_map(bi, t):
        inside = (t > 0) & (t < n_real)
        return (bi, jnp.where(inside, t * hpt - 1, seq // CONV_HALO), 0)

    const = lambda bi, t: (0, 0)
    return pl.pallas_call(
        functools.partial(_conv_kernel, n_real_tiles=n_real),
        out_shape=jax.ShapeDtypeStruct((bsz, lp, c), BF16),
        grid=(bsz, n_real + 1),
        in_specs=[
            pl.BlockSpec((1, tc, c), lambda bi, t: (bi, t, 0)),
            pl.BlockSpec((1, CONV_HALO, c), halo_map),
            pl.BlockSpec(w.shape, const),
            pl.BlockSpec((1, c), const),
            pl.BlockSpec((1, c), const),
            pl.BlockSpec((1, c), const),
        ],
        out_specs=pl.BlockSpec((1, tc, c), lambda bi, t: (bi, t, 0)),
        scratch_shapes=[pltpu.VMEM((CONV_HALO + tc, c), F32)],
        compiler_params=_params("parallel", "arbitrary"),
        name="conv_ln_swish",
    )(a, a, w, b, g, beta)


def _softmax_step(s, v, m_ref, l_ref, acc_ref, i):
    m_prev = m_ref[i]
    m_new = jnp.maximum(m_prev, jnp.max(s, axis=1, keepdims=True))
    alpha = jnp.exp(m_prev - m_new)
    p = jnp.exp(s - jnp.concatenate([m_new] * (s.shape[1] // LANE), axis=1))
    l_ref[i] = alpha * l_ref[i] + jnp.sum(p, axis=1, keepdims=True)
    acc_ref[i] = alpha * acc_ref[i] + _dot(p.astype(BF16), v)
    m_ref[i] = m_new


def _init_state(m_ref, l_ref, acc_ref):
    m_ref[...] = jnp.full(m_ref.shape, NEG_INF, F32)
    l_ref[...] = jnp.zeros(l_ref.shape, F32)
    acc_ref[...] = jnp.zeros(acc_ref.shape, F32)


def _split_diff_q(q):
    lane = lax.broadcasted_iota(jnp.int32, q.shape, 1)
    qs = q * jnp.asarray(DIFF_HEAD_DIM ** -0.5, q.dtype)
    zero = jnp.zeros_like(qs)
    return jnp.where(lane < DIFF_HEAD_DIM, qs, zero), jnp.where(lane < DIFF_HEAD_DIM, zero, qs)


def _diff_lambda(lq1, lk1, lq2, lk2, lam_init):
    return (jnp.exp(jnp.sum(lq1 * lk1, axis=-1, keepdims=True))
            - jnp.exp(jnp.sum(lq2 * lk2, axis=-1, keepdims=True)) + lam_init)


def _diff_finish(o1, o2, lam, subln, lam_init):
    o = o1 - lam * o2
    return _rms(o, subln) * (1.0 - lam_init)


def _causal_sweep(qi, scores, update):
    npairs = jnp.where(qi > 0, lax.shift_right_logical(qi - 1, 1), 0)
    scores(0, 0)

    def pair(u, carry):
        t = 2 * u
        scores(t + 1, 1)
        update(t, 0, False)
        scores(t + 2, 0)
        update(t + 1, 1, False)
        return carry

    lax.fori_loop(0, npairs, pair, 0)

    t0 = 2 * npairs
    scores(jnp.minimum(t0 + 1, qi), 1)
    update(t0, 0, True)

    @pl.when(t0 + 1 <= qi)
    def _():
        scores(jnp.minimum(t0 + 2, qi), 0)
        update(t0 + 1, 1, True)

    @pl.when(t0 + 2 <= qi)
    def _():
        update(t0 + 2, 0, True)


def _near_tile(t, qi):
    return jnp.clip(t - qi + 2, 0, 2)


def _diff_kernel(q_ref, k_ref, v_ref, km_ref, vm_ref, bn_ref, bm_ref,
                 lq1_ref, lk1_ref, lq2_ref, lk2_ref, sub_ref, li_ref,
                 o_ref, m_ref, l_ref, acc_ref, s_ref):
    qi = pl.program_id(2)
    tq = q_ref.shape[1]
    q1, q2 = _split_diff_q(q_ref[0])
    _init_state(m_ref, l_ref, acc_ref)

    km = km_ref[0]
    vm = vm_ref[0]
    bm = bm_ref[0, 0]
    _softmax_step(_dot_nt(q1, km) + bm, vm, m_ref, l_ref, acc_ref, 0)
    _softmax_step(_dot_nt(q2, km) + bm, vm, m_ref, l_ref, acc_ref, 1)

    def scores(t, slot):
        k = k_ref[0, pl.ds(pl.multiple_of(t * tq, tq), tq), :]
        s_ref[slot, 0] = _dot_nt(q1, k)
        s_ref[slot, 1] = _dot_nt(q2, k)

    def update(t, slot, biased):
        v = v_ref[0, pl.ds(pl.multiple_of(t * tq, tq), tq), :]
        for i in range(2):
            s = s_ref[slot, i]
            if biased:
                s = s + bn_ref[0, _near_tile(t, qi)]
            _softmax_step(s, v, m_ref, l_ref, acc_ref, i)

    _causal_sweep(qi, scores, update)

    li = li_ref[...]
    lam = _diff_lambda(lq1_ref[...], lk1_ref[...], lq2_ref[...], lk2_ref[...], li)
    o = _diff_finish(acc_ref[0] / l_ref[0], acc_ref[1] / l_ref[1], lam, sub_ref[...], li)
    o_ref[0] = o.astype(o_ref.dtype)


def _diff_meta_kernel(q_ref, k_ref, v_ref, bm_ref, lq1_ref, lk1_ref, lq2_ref, lk2_ref, sub_ref, li_ref,
                      prev_ref, o_ref):
    del prev_ref
    li = li_ref[...]
    lam = _diff_lambda(lq1_ref[...], lk1_ref[...], lq2_ref[...], lk2_ref[...], li)
    for hd in range(DIFF_HEADS):
        sl = slice(hd * LANE, (hd + 1) * LANE)
        q1, q2 = _split_diff_q(q_ref[0, :, sl])
        k = k_ref[0, :, sl]
        v = v_ref[0, :, sl]
        outs = []
        for qz in (q1, q2):
            s = _dot_nt(qz, k) + bm_ref[hd]
            p = jnp.exp(s - jnp.max(s, axis=1, keepdims=True))
            outs.append(_dot(p.astype(BF16), v) / jnp.sum(p, axis=1, keepdims=True))
        o_ref[0, :, sl] = _diff_finish(outs[0], outs[1], lam, sub_ref[...], li).astype(o_ref.dtype)


def _diff_attention(qkv, bias, lq1, lk1, lq2, lk2, subln, lam_init, seq, tq):
    bsz, lp, _ = qkv.shape
    nq = seq // tq
    mb = seq // META_ROWS
    hq = DIFF_HEADS
    vec = lambda n: pl.BlockSpec((1, n), lambda *_: (0, 0))
    small = [vec(DIFF_HEAD_DIM)] * 4 + [vec(DIFF_V_DIM), vec(1)]
    smalls = (lq1, lk1, lq2, lk2, subln, lam_init)
    out = pl.pallas_call(
        _diff_kernel,
        out_shape=jax.ShapeDtypeStruct((bsz, lp, hq * DIFF_V_DIM), BF16),
        grid=(bsz, hq, nq),
        in_specs=[
            pl.BlockSpec((1, tq, LANE), lambda b, h, i: (b, i, h)),
            pl.BlockSpec((1, seq, LANE), lambda b, h, i: (b, 0, hq + h)),
            pl.BlockSpec((1, seq, LANE), lambda b, h, i: (b, 0, 2 * hq + h)),
            pl.BlockSpec((1, META_ROWS, LANE), lambda b, h, i: (b, mb, hq + h)),
            pl.BlockSpec((1, META_ROWS, LANE), lambda b, h, i: (b, mb, 2 * hq + h)),
            pl.BlockSpec((1, 3, tq, tq), lambda b, h, i: (h, 0, 0, 0)),
            pl.BlockSpec((1, 1, tq, META_ROWS), lambda b, h, i: (h, jnp.minimum(i, 1), 0, 0)),
        ] + small,
        out_specs=pl.BlockSpec((1, tq, LANE), lambda b, h, i: (b, i, h)),
        scratch_shapes=[pltpu.VMEM((2, tq, LANE), F32), pltpu.VMEM((2, tq, LANE), F32),
                        pltpu.VMEM((2, tq, DIFF_V_DIM), F32), pltpu.VMEM((2, 2, tq, tq), F32)],
        compiler_params=_params("parallel", "parallel", "arbitrary"),
        name="diff_attention",
    )(qkv, qkv, qkv, qkv, qkv, bias["near"], bias["meta_keys"], *smalls)

    wide = hq * LANE
    return pl.pallas_call(
        _diff_meta_kernel,
        out_shape=jax.ShapeDtypeStruct(out.shape, out.dtype),
        grid=(bsz,),
        in_specs=[
            pl.BlockSpec((1, META_ROWS, wide), lambda b: (b, mb, 0)),
            pl.BlockSpec((1, META_ROWS, wide), lambda b: (b, mb, 1)),
            pl.BlockSpec((1, META_ROWS, wide), lambda b: (b, mb, 2)),
            pl.BlockSpec(bias["meta_meta"].shape, lambda b: (0, 0, 0)),
        ] + small + [pl.BlockSpec(memory_space=pl.ANY)],
        out_specs=pl.BlockSpec((1, META_ROWS, wide), lambda b: (b, mb, 0)),
        input_output_aliases={10: 0},
        compiler_params=_params("parallel"),
        name="diff_attention_meta",
    )(qkv, qkv, qkv, bias["meta_meta"], *smalls, out)


def _mla_kernel(q_ref, k_ref, v_ref, km_ref, vm_ref, md_ref, mm_ref, o_ref, m_ref, l_ref, acc_ref, s_ref,
                *, scale):
    qi = pl.program_id(2)
    tq = q_ref.shape[1]
    q = q_ref[0]
    _init_state(m_ref, l_ref, acc_ref)

    _softmax_step(_dot_nt(q, km_ref[0]) * scale + mm_ref[...], vm_ref[0], m_ref, l_ref, acc_ref, 0)

    def scores(t, slot):
        k = k_ref[0, pl.ds(pl.multiple_of(t * tq, tq), tq), :]
        s_ref[slot] = _dot_nt(q, k) * scale

    def update(t, slot, masked):
        v = v_ref[0, pl.ds(pl.multiple_of(t * tq, tq), tq), :]
        s = s_ref[slot]
        if masked:
            s = s + md_ref[_near_tile(t, qi)]
        _softmax_step(s, v, m_ref, l_ref, acc_ref, 0)

    _causal_sweep(qi, scores, update)
    o_ref[0] = (acc_ref[0] / l_ref[0]).astype(o_ref.dtype)


def _mla_meta_kernel(q_ref, k_ref, v_ref, mm_ref, prev_ref, o_ref, *, scale):
    del prev_ref
    for hd in range(MLA_HEADS):
        qk = slice(hd * MLA_QK_PAD, (hd + 1) * MLA_QK_PAD)
        vs = slice(hd * MLA_V_DIM, (hd + 1) * MLA_V_DIM)
        s = _dot_nt(q_ref[0, :, qk], k_ref[0, :, qk]) * scale + mm_ref[...]
        p = jnp.exp(s - jnp.max(s, axis=1, keepdims=True))
        o = _dot(p.astype(BF16), v_ref[0, :, vs]) / jnp.sum(p, axis=1, keepdims=True)
        o_ref[0, :, vs] = o.astype(o_ref.dtype)


def _mla_attention(qm, km, vm, masks, seq, tq):
    bsz, lp, _ = qm.shape
    nq = seq // tq
    mb = seq // META_ROWS
    scale = (MLA_NOPE_DIM + MLA_ROPE_DIM) ** -0.5
    out = pl.pallas_call(
        functools.partial(_mla_kernel, scale=scale),
        out_shape=jax.ShapeDtypeStruct((bsz, lp, MLA_HEADS * MLA_V_DIM), BF16),
        grid=(bsz, MLA_HEADS, nq),
        in_specs=[
            pl.BlockSpec((1, tq, MLA_QK_PAD), lambda b, h, i: (b, i, h)),
            pl.BlockSpec((1, seq, MLA_QK_PAD), lambda b, h, i: (b, 0, h)),
            pl.BlockSpec((1, seq, MLA_V_DIM), lambda b, h, i: (b, 0, h)),
            pl.BlockSpec((1, META_ROWS, MLA_QK_PAD), lambda b, h, i: (b, mb, h)),
            pl.BlockSpec((1, META_ROWS, MLA_V_DIM), lambda b, h, i: (b, mb, h)),
            pl.BlockSpec((3, tq, tq), lambda b, h, i: (0, 0, 0)),
            pl.BlockSpec((1, META_ROWS), lambda b, h, i: (0, 0)),
        ],
        out_specs=pl.BlockSpec((1, tq, MLA_V_DIM), lambda b, h, i: (b, i, h)),
        scratch_shapes=[pltpu.VMEM((1, tq, LANE), F32), pltpu.VMEM((1, tq, LANE), F32),
                        pltpu.VMEM((1, tq, MLA_V_DIM), F32), pltpu.VMEM((2, tq, tq), F32)],
        compiler_params=_params("parallel", "parallel", "arbitrary"),
        name="mla_attention",
    )(qm, km, vm, km, vm, masks["near"], masks["meta_cols"])

    return pl.pallas_call(
        functools.partial(_mla_meta_kernel, scale=scale),
        out_shape=jax.ShapeDtypeStruct(out.shape, out.dtype),
        grid=(bsz,),
        in_specs=[
            pl.BlockSpec((1, META_ROWS, MLA_HEADS * MLA_QK_PAD), lambda b: (b, mb, 0)),
            pl.BlockSpec((1, META_ROWS, MLA_HEADS * MLA_QK_PAD), lambda b: (b, mb, 0)),
            pl.BlockSpec((1, META_ROWS, MLA_HEADS * MLA_V_DIM), lambda b: (b, mb, 0)),
            pl.BlockSpec((1, META_ROWS), lambda b: (0, 0)),
            pl.BlockSpec(memory_space=pl.ANY),
        ],
        out_specs=pl.BlockSpec((1, META_ROWS, MLA_HEADS * MLA_V_DIM), lambda b: (b, mb, 0)),
        input_output_aliases={4: 0},
        compiler_params=_params("parallel"),
        name="mla_attention_meta",
    )(qm, km, vm, masks["meta_cols"], out)


def _mix_kernel(h_ref, g_ref, xa_ref, xb_ref, xc_ref, ga_ref, gb_ref, gc_ref, wa_ref, wb_ref, wc_ref,
                o_ref, hn_ref):
    @pl.when(pl.program_id(1) == 0)
    def _():
        hn_ref[...] = _rms(h_ref[...], g_ref[...]).astype(BF16)

    hn = hn_ref[...]
    mixed = jax.nn.sigmoid(_dot(hn, ga_ref[...])) * _dot(xa_ref[...], wa_ref[...])
    mixed = mixed + jax.nn.sigmoid(_dot(hn, gb_ref[...])) * _dot(xb_ref[...], wb_ref[...])
    mixed = mixed + jax.nn.sigmoid(_dot(hn, gc_ref[...])) * _dot(xc_ref[...], wc_ref[...])
    o_ref[...] = mixed.astype(o_ref.dtype)


def _gated_mix(h, g, xa, xb, xc, w_gates, wa, wb, wc, tm, tn):
    t, d = h.shape
    nj = d // tn
    branch = lambda x: pl.BlockSpec((tm, x.shape[1]), lambda i, j: (i, 0))
    gate = lambda n: pl.BlockSpec((d, tn), lambda i, j: (0, n * nj + j))
    wbr = lambda w: pl.BlockSpec((w.shape[0], tn), lambda i, j: (0, j))
    return pl.pallas_call(
        _mix_kernel,
        out_shape=jax.ShapeDtypeStruct((t, d), BF16),
        grid=(t // tm, nj),
        in_specs=[
            pl.BlockSpec((tm, d), lambda i, j: (i, 0)),
            pl.BlockSpec((1, d), lambda i, j: (0, 0)),
            branch(xa), branch(xb), branch(xc),
            gate(0), gate(1), gate(2),
            wbr(wa), wbr(wb), wbr(wc),
        ],
        out_specs=pl.BlockSpec((tm, tn), lambda i, j: (i, j)),
        scratch_shapes=[pltpu.VMEM((tm, d), BF16)],
        compiler_params=_params("parallel", "arbitrary"),
        name="gated_mix",
    )(h, g, xa, xb, xc, w_gates, w_gates, w_gates, wa, wb, wc)


def _residual_mm_kernel(h_ref, x_ref, w_ref, o_ref):
    o_ref[...] = h_ref[...] + _dot(x_ref[...], w_ref[...])


def _residual_matmul(h, x, w, tm, tn):
    t, d = h.shape
    k = x.shape[1]
    return pl.pallas_call(
        _residual_mm_kernel,
        out_shape=jax.ShapeDtypeStruct((t, d), F32),
        grid=(t // tm, d // tn),
        in_specs=[
            pl.BlockSpec((tm, tn), lambda i, j: (i, j)),
            pl.BlockSpec((tm, k), lambda i, j: (i, 0)),
            pl.BlockSpec((k, tn), lambda i, j: (0, j)),
        ],
        out_specs=pl.BlockSpec((tm, tn), lambda i, j: (i, j)),
        compiler_params=_params("parallel", "arbitrary"),
        name="out_proj",
    )(h, x, w)


def _ffn_kernel(h_ref, g_ref, wg_ref, wu_ref, wd_ref, o_ref, hn_ref):
    f = pl.program_id(1)

    @pl.when(f == 0)
    def _():
        hn_ref[...] = _rms(h_ref[...], g_ref[...]).astype(BF16)

    hn = hn_ref[...]
    gate = _dot(hn, wg_ref[...])
    act = (gate * jax.nn.sigmoid(gate) * _dot(hn, wu_ref[...])).astype(BF16)
    part = _dot(act, wd_ref[...])

    @pl.when(f == 0)
    def _():
        o_ref[...] = h_ref[...] + part

    @pl.when(f > 0)
    def _():
        o_ref[...] += part


def _ffn(h, g, wg, wu, wd, tm, tf):
    t, d = h.shape
    dff = wg.shape[1]
    return pl.pallas_call(
        _ffn_kernel,
        out_shape=jax.ShapeDtypeStruct((t, d), F32),
        grid=(t // tm, dff // tf),
        in_specs=[
            pl.BlockSpec((tm, d), lambda i, f: (i, 0)),
            pl.BlockSpec((1, d), lambda i, f: (0, 0)),
            pl.BlockSpec((d, tf), lambda i, f: (0, f)),
            pl.BlockSpec((d, tf), lambda i, f: (0, f)),
            pl.BlockSpec((tf, d), lambda i, f: (f, 0)),
        ],
        out_specs=pl.BlockSpec((tm, d), lambda i, f: (i, 0)),
        scratch_shapes=[pltpu.VMEM((tm, d), BF16)],
        compiler_params=_params("parallel", "arbitrary"),
        name="swiglu_ffn",
    )(h, g, wg, wu, wd)


def _final_norm_kernel(h_ref, g_ref, o_ref):
    o_ref[0] = _rms(h_ref[0], g_ref[...])


def _final_norm(h3, g, seq, tr):
    bsz, _, d = h3.shape
    return pl.pallas_call(
        _final_norm_kernel,
        out_shape=jax.ShapeDtypeStruct((bsz, seq, d), F32),
        grid=(bsz, seq // tr),
        in_specs=[
            pl.BlockSpec((1, tr, d), lambda b, i: (b, i, 0)),
            pl.BlockSpec((1, d), lambda b, i: (0, 0)),
        ],
        out_specs=pl.BlockSpec((1, tr, d), lambda b, i: (b, i, 0)),
        compiler_params=_params("parallel", "parallel"),
        name="final_norm",
    )(h3, g)


def _t5_bucket(rel):
    nb = REL_BUCKETS // 2
    ret = jnp.where(rel > 0, nb, 0)
    n = jnp.abs(rel)
    max_exact = nb // 2
    nf = jnp.maximum(n, 1).astype(F32)
    large = max_exact + (jnp.log(nf / max_exact) / math.log(REL_MAX_DIST / max_exact)
                         * (nb - max_exact)).astype(jnp.int32)
    large = jnp.minimum(large, nb - 1)
    return ret + jnp.where(n < max_exact, n, large)


def _position_tables(rel_bias, seq, tq):
    far = rel_bias[_t5_bucket(jnp.asarray(-REL_MAX_DIST, jnp.int32))]
    table = (rel_bias - far[None, :]).astype(F32)

    def bias(rel, visible):
        bucket = _t5_bucket(rel)[None]
        t = jnp.zeros((DIFF_HEADS,) + rel.shape, F32)
        for b in range(REL_BUCKETS):
            t = t + jnp.where(bucket == b, table[b][:, None, None], 0.0)
        return jnp.where(visible[None], t, NEG_INF)

    r = jnp.arange(tq, dtype=jnp.int32)[:, None]
    c = jnp.arange(tq, dtype=jnp.int32)[None, :]
    chunk_vis = (c // CHUNK) <= (r // CHUNK)
    all_vis = jnp.ones((tq, tq), bool)
    m = jnp.arange(META_ROWS, dtype=jnp.int32)
    meta_vis = jnp.broadcast_to((m < N_META)[None, :], (tq, META_ROWS))
    diff = {
        "near": jnp.stack([jnp.zeros((DIFF_HEADS, tq, tq), F32), bias(c - tq - r, all_vis),
                           bias(c - r, chunk_vis)], axis=1),
        "meta_keys": jnp.stack([bias(m[None, :] - N_META - r, meta_vis),
                                bias(jnp.full((tq, META_ROWS), -REL_MAX_DIST - tq, jnp.int32), meta_vis)],
                               axis=1),
        "meta_meta": bias(m[None, :] - m[:, None],
                          jnp.broadcast_to((m < N_META)[None, :], (META_ROWS, META_ROWS))),
    }
    mla = {
        "near": jnp.stack([jnp.zeros((tq, tq), F32), jnp.zeros((tq, tq), F32),
                           jnp.where(chunk_vis, 0.0, NEG_INF).astype(F32)]),
        "meta_cols": jnp.where(m < N_META, 0.0, NEG_INF).astype(F32)[None, :],
    }
    return diff, mla


def _rope_tables(seq):
    half = MLA_ROPE_DIM // 2
    row = jnp.arange(seq + META_ROWS, dtype=jnp.int32)
    pos = jnp.where(row < seq, row + N_META, row - seq)
    inv = ROPE_THETA ** (-jnp.arange(half, dtype=F32) / half)
    ang = pos.astype(F32)[:, None] * inv[None, :]
    cos, sin = jnp.cos(ang), jnp.sin(ang)
    pad = jnp.zeros((seq + META_ROWS, LANE - MLA_ROPE_DIM), F32)
    return (jnp.concatenate([cos, cos, pad], axis=1), jnp.concatenate([-sin, sin, pad], axis=1))


def _swap_halves(w):
    half = w.shape[-1] // 2
    return jnp.concatenate([w[..., half:], w[..., :half]], axis=-1)


def _lane_pad(w):
    return jnp.concatenate([w, jnp.zeros(w.shape[:-1] + (LANE - w.shape[-1],), w.dtype)], axis=-1)


def kernel(x, meta_tokens, rel_bias, norm_mix, w_in, conv_w, conv_b, conv_ln_g, conv_ln_b, w_br_conv,
           diff_lq1, diff_lk1, diff_lq2, diff_lk2, diff_subln, w_br_diff, mla_q_norm, w_uq, mla_kv_norm,
           w_ukv, w_br_mla, w_out, norm_ffn, w_ffn_gate, w_ffn_up, w_ffn_down, final_norm):
    bsz, seq, d = x.shape
    depth = w_in.shape[0]
    dff = w_ffn_gate.shape[-1]
    lp = seq + META_ROWS
    t = bsz * lp
    assert seq % META_ROWS == 0 and d % LANE == 0

    tq = _tile(seq, 512)
    tm = _tile(lp, 640)
    tn = _tile(d, 512)
    tf = _tile(dff, 512)
    tn_mix = _tile(d, 256)

    w_in_b = w_in.astype(BF16)
    w_val = w_in_b[:, :, OFF_A:OFF_A + CONV_WIDTH]
    w_gate = w_in_b[:, :, OFF_A + CONV_WIDTH:OFF_BQ]
    w_qkv = w_in_b[:, :, OFF_BQ:OFF_CQ]
    w_kr = w_in_b[:, :, OFF_CR:OFF_G]
    w_c = jnp.concatenate([w_in_b[:, :, OFF_CQ:OFF_CR], _lane_pad(w_kr), _lane_pad(_swap_halves(w_kr))], axis=-1)
    w_gates = w_in_b[:, :, OFF_G:]
    uq = w_uq.astype(BF16).reshape(depth, MLA_Q_RANK, MLA_HEADS, MLA_NOPE_DIM + MLA_ROPE_DIM)
    uq_rope = uq[..., MLA_NOPE_DIM:]
    w_uq_p = jnp.concatenate([uq[..., :MLA_NOPE_DIM], _lane_pad(uq_rope), _lane_pad(_swap_halves(uq_rope))],
                             axis=-1).reshape(depth, MLA_Q_RANK, MLA_HEADS * MLA_UQ_PAD)
    w_ukv_b = w_ukv.astype(BF16)
    w_bra, w_brb, w_brc = w_br_conv.astype(BF16), w_br_diff.astype(BF16), w_br_mla.astype(BF16)
    w_out_b = w_out.astype(BF16)
    w_fg, w_fu, w_fd = w_ffn_gate.astype(BF16), w_ffn_up.astype(BF16), w_ffn_down.astype(BF16)

    diff_tabs, mla_tabs = _position_tables(rel_bias, seq, tq)
    cos_t, sin_t = _rope_tables(seq)

    row = lambda v: v.reshape(1, -1)
    h = jnp.concatenate([x, jnp.broadcast_to(meta_tokens[None].astype(x.dtype), (bsz, N_META, d)),
                         jnp.zeros((bsz, META_ROWS - N_META, d), x.dtype)], axis=1).reshape(t, d)

    for l in range(depth):
        lam_init = jnp.full((1, 1), 0.8 - 0.6 * math.exp(-0.3 * l), F32)
        g_mix = row(norm_mix[l])
        a = _norm_glu(h, g_mix, w_val[l], w_gate[l], tm, tn)
        qkv = _norm_matmul(h, g_mix, w_qkv[l], BF16, tm, tn)
        qm, km, vm = _mla_proj(h, g_mix, w_c[l], row(mla_q_norm[l]), row(mla_kv_norm[l]), w_uq_p[l],
                               w_ukv_b[l], cos_t, sin_t, tm)
        ya = _conv_branch(a.reshape(bsz, lp, CONV_WIDTH), conv_w[l], row(conv_b[l]), row(conv_ln_g[l]),
                          row(conv_ln_b[l]), seq)
        yb = _diff_attention(qkv.reshape(bsz, lp, -1), diff_tabs, row(diff_lq1[l]), row(diff_lk1[l]),
                             row(diff_lq2[l]), row(diff_lk2[l]), row(diff_subln[l]), lam_init, seq, tq)
        yc = _mla_attention(qm.reshape(bsz, lp, -1), km.reshape(bsz, lp, -1), vm.reshape(bsz, lp, -1),
                            mla_tabs, seq, tq)
        mixed = _gated_mix(h, g_mix, ya.reshape(t, -1), yb.reshape(t, -1), yc.reshape(t, -1),
                           w_gates[l], w_bra[l], w_brb[l], w_brc[l], tm, tn_mix)
        h = _residual_matmul(h, mixed, w_out_b[l], tm, tn)
        h = _ffn(h, row(norm_ffn[l]), w_fg[l], w_fu[l], w_fd[l], tm, tf)

    return _final_norm(h.reshape(bsz, lp, d), row(final_norm), seq, _tile(seq, 512))
```

```python
import functools
import math

import jax
import jax.numpy as jnp
from jax import lax
from jax.experimental import pallas as pl
from jax.experimental.pallas import tpu as pltpu

F32 = jnp.float32
BF16 = jnp.bfloat16

CHUNK = 64
N_META = 16
NORM_EPS = 1e-6
NEG_INF = -1e30

CONV_WIDTH = 1024
CONV_KERNEL = 31
DIFF_HEADS = 8
DIFF_HEAD_DIM = 64
DIFF_V_DIM = 2 * DIFF_HEAD_DIM
REL_BUCKETS = 32
REL_MAX_DIST = 128
MLA_HEADS = 8
MLA_Q_RANK = 512
MLA_KV_RANK = 256
MLA_NOPE_DIM = 128
MLA_ROPE_DIM = 64
MLA_V_DIM = 128
ROPE_THETA = 10000.0

DIFF_QK = DIFF_HEADS * 2 * DIFF_HEAD_DIM
OFF_A = 0
OFF_BQ = OFF_A + 2 * CONV_WIDTH
OFF_BK = OFF_BQ + DIFF_QK
OFF_BV = OFF_BK + DIFF_QK
OFF_CQ = OFF_BV + DIFF_HEADS * DIFF_V_DIM
OFF_CKV = OFF_CQ + MLA_Q_RANK
OFF_CR = OFF_CKV + MLA_KV_RANK
OFF_G = OFF_CR + MLA_ROPE_DIM

LANE = 128
META_ROWS = 128
MLA_QK_PAD = 2 * LANE
MLA_UQ_PAD = 3 * LANE
VMEM_LIMIT = 56 * 1024 * 1024


def _params(*sem):
    return pltpu.CompilerParams(dimension_semantics=sem, vmem_limit_bytes=VMEM_LIMIT)


def _tile(n, cap, unit=LANE):
    best = None
    for t in range(unit, min(n, cap) + 1, unit):
        if n % t == 0:
            best = t
    assert best is not None, (n, cap, unit)
    return best


def _rms(x, g):
    ms = jnp.mean(x * x, axis=-1, keepdims=True)
    return x * lax.rsqrt(ms + NORM_EPS) * g


def _dot(a, b):
    return jnp.dot(a, b, preferred_element_type=F32)


def _dot_nt(a, b):
    return lax.dot_general(a, b, (((1,), (1,)), ((), ())), preferred_element_type=F32)


def _norm_mm_kernel(h_ref, g_ref, w_ref, o_ref, hn_ref):
    @pl.when(pl.program_id(1) == 0)
    def _():
        hn_ref[...] = _rms(h_ref[...], g_ref[...]).astype(BF16)

    o_ref[...] = _dot(hn_ref[...], w_ref[...]).astype(o_ref.dtype)


def _norm_matmul(h, g, w, out_dtype, tm, tn):
    t, d = h.shape
    n = w.shape[1]
    return pl.pallas_call(
        _norm_mm_kernel,
        out_shape=jax.ShapeDtypeStruct((t, n), out_dtype),
        grid=(t // tm, n // tn),
        in_specs=[
            pl.BlockSpec((tm, d), lambda i, j: (i, 0)),
            pl.BlockSpec((1, d), lambda i, j: (0, 0)),
            pl.BlockSpec((d, tn), lambda i, j: (0, j)),
        ],
        out_specs=pl.BlockSpec((tm, tn), lambda i, j: (i, j)),
        scratch_shapes=[pltpu.VMEM((tm, d), BF16)],
        compiler_params=_params("parallel", "arbitrary"),
        name="norm_qkv",
    )(h, g, w)


def _glu_kernel(h_ref, g_ref, wv_ref, wg_ref, o_ref, hn_ref):
    @pl.when(pl.program_id(1) == 0)
    def _():
        hn_ref[...] = _rms(h_ref[...], g_ref[...]).astype(BF16)

    hn = hn_ref[...]
    o_ref[...] = _dot(hn, wv_ref[...]) * jax.nn.sigmoid(_dot(hn, wg_ref[...]))


def _norm_glu(h, g, w_val, w_gate, tm, tn):
    t, d = h.shape
    n = w_val.shape[1]
    return pl.pallas_call(
        _glu_kernel,
        out_shape=jax.ShapeDtypeStruct((t, n), F32),
        grid=(t // tm, n // tn),
        in_specs=[
            pl.BlockSpec((tm, d), lambda i, j: (i, 0)),
            pl.BlockSpec((1, d), lambda i, j: (0, 0)),
            pl.BlockSpec((d, tn), lambda i, j: (0, j)),
            pl.BlockSpec((d, tn), lambda i, j: (0, j)),
        ],
        out_specs=pl.BlockSpec((tm, tn), lambda i, j: (i, j)),
        scratch_shapes=[pltpu.VMEM((tm, d), BF16)],
        compiler_params=_params("parallel", "arbitrary"),
        name="norm_glu",
    )(h, g, w_val, w_gate)


def _mla_proj_kernel(h_ref, g_ref, wc_ref, gq_ref, gkv_ref, wuq_ref, wukv_ref, cos_ref, sin_ref,
                     q_ref, k_ref, v_ref):
    hn = _rms(h_ref[...], g_ref[...]).astype(BF16)
    c = _dot(hn, wc_ref[...])
    cos = cos_ref[...]
    sin = sin_ref[...]
    o_kr = MLA_Q_RANK + MLA_KV_RANK
    kr = (c[:, o_kr:o_kr + LANE] * cos + c[:, o_kr + LANE:o_kr + 2 * LANE] * sin).astype(BF16)
    cqn = _rms(c[:, :MLA_Q_RANK], gq_ref[...]).astype(BF16)
    ckvn = _rms(c[:, MLA_Q_RANK:o_kr], gkv_ref[...]).astype(BF16)
    for hd in range(MLA_HEADS):
        q = _dot(cqn, wuq_ref[:, hd * MLA_UQ_PAD:(hd + 1) * MLA_UQ_PAD])
        kv = _dot(ckvn, wukv_ref[:, hd * 2 * LANE:(hd + 1) * 2 * LANE])
        qo = hd * MLA_QK_PAD
        q_ref[:, qo:qo + LANE] = q[:, :LANE].astype(BF16)
        q_ref[:, qo + LANE:qo + 2 * LANE] = (q[:, LANE:2 * LANE] * cos + q[:, 2 * LANE:] * sin).astype(BF16)
        k_ref[:, qo:qo + LANE] = kv[:, :LANE].astype(BF16)
        k_ref[:, qo + LANE:qo + 2 * LANE] = kr
        v_ref[:, hd * LANE:(hd + 1) * LANE] = kv[:, LANE:].astype(BF16)


def _mla_proj(h, g, wc, gq, gkv, wuq, wukv, cos_t, sin_t, tm):
    t, d = h.shape
    lp = cos_t.shape[0]
    nper = lp // tm
    const = lambda i: (0, 0)
    return pl.pallas_call(
        _mla_proj_kernel,
        out_shape=(
            jax.ShapeDtypeStruct((t, MLA_HEADS * MLA_QK_PAD), BF16),
            jax.ShapeDtypeStruct((t, MLA_HEADS * MLA_QK_PAD), BF16),
            jax.ShapeDtypeStruct((t, MLA_HEADS * MLA_V_DIM), BF16),
        ),
        grid=(t // tm,),
        in_specs=[
            pl.BlockSpec((tm, d), lambda i: (i, 0)),
            pl.BlockSpec((1, d), const),
            pl.BlockSpec(wc.shape, const),
            pl.BlockSpec(gq.shape, const),
            pl.BlockSpec(gkv.shape, const),
            pl.BlockSpec(wuq.shape, const),
            pl.BlockSpec(wukv.shape, const),
            pl.BlockSpec((tm, LANE), lambda i: (i % nper, 0)),
            pl.BlockSpec((tm, LANE), lambda i: (i % nper, 0)),
        ],
        out_specs=(
            pl.BlockSpec((tm, MLA_HEADS * MLA_QK_PAD), lambda i: (i, 0)),
            pl.BlockSpec((tm, MLA_HEADS * MLA_QK_PAD), lambda i: (i, 0)),
            pl.BlockSpec((tm, MLA_HEADS * MLA_V_DIM), lambda i: (i, 0)),
        ),
        compiler_params=_params("parallel"),
        name="mla_proj",
    )(h, g, wc, gq, gkv, wuq, wukv, cos_t, sin_t)


CONV_HALO = 32
CONV_ROWS = 32
SUBLANES = 8


def _conv_kernel(cur_ref, halo_ref, w_ref, b_ref, g_ref, beta_ref, o_ref, xs_ref, xr_ref, *, n_real_tiles):
    t = pl.program_id(1)
    tc = cur_ref.shape[1]
    c = cur_ref.shape[2]

    @pl.when(t == n_real_tiles)
    def _():
        xs_ref[0:CONV_HALO, :] = jnp.zeros((CONV_HALO, c), F32)

    @pl.when(t == 0)
    def _():
        xs_ref[0:CONV_HALO - N_META, :] = jnp.zeros((CONV_HALO - N_META, c), F32)
        xs_ref[CONV_HALO - N_META:CONV_HALO, :] = halo_ref[0, 0:N_META, :]

    @pl.when((t > 0) & (t < n_real_tiles))
    def _():
        xs_ref[0:CONV_HALO, :] = halo_ref[0]

    xs_ref[CONV_HALO:, :] = cur_ref[0]

    n_shift = CONV_HALO + tc - SUBLANES
    for r in range(1, SUBLANES):
        xr_ref[r - 1, 0:n_shift, :] = xs_ref[r:r + n_shift, :]

    first_tap = CONV_HALO - (CONV_KERNEL - 1)
    for rc in range(tc // CONV_ROWS):
        base = rc * CONV_ROWS
        acc = jnp.broadcast_to(b_ref[...], (CONV_ROWS, c))
        for k in range(CONV_KERNEL):
            shift = (first_tap + k) % SUBLANES
            lo = base + (first_tap + k) - shift
            if shift == 0:
                x = xs_ref[lo:lo + CONV_ROWS, :]
            else:
                x = xr_ref[shift - 1, lo:lo + CONV_ROWS, :]
            acc = acc + w_ref[k:k + 1, :] * x
        mu = jnp.mean(acc, axis=-1, keepdims=True)
        ctr = acc - mu
        var = jnp.mean(ctr * ctr, axis=-1, keepdims=True)
        y = ctr * lax.rsqrt(var + NORM_EPS) * g_ref[...] + beta_ref[...]
        o_ref[0, base:base + CONV_ROWS, :] = (y * jax.nn.sigmoid(y)).astype(o_ref.dtype)


def _conv_branch(a, w, b, g, beta, seq):
    bsz, lp, c = a.shape
    tc = META_ROWS
    n_real = seq // tc
    hpt = tc // CONV_HALO

    def halo_map(bi, t):
        inside = (t > 0) & (t < n_real)
        return (bi, jnp.where(inside, t * hpt - 1, seq // CONV_HALO), 0)

    const = lambda bi, t: (0, 0)
    return pl.pallas_call(
        functools.partial(_conv_kernel, n_real_tiles=n_real),
        out_shape=jax.ShapeDtypeStruct((bsz, lp, c), BF16),
        grid=(bsz, n_real + 1),
        in_specs=[
            pl.BlockSpec((1, tc, c), lambda bi, t: (bi, t, 0)),
            pl.BlockSpec((1, CONV_HALO, c), halo_map),
            pl.BlockSpec(w.shape, const),
            pl.BlockSpec((1, c), const),
            pl.BlockSpec((1, c), const),
            pl.BlockSpec((1, c), const),
        ],
        out_specs=pl.BlockSpec((1, tc, c), lambda bi, t: (bi, t, 0)),
        scratch_shapes=[pltpu.VMEM((CONV_HALO + tc, c), F32),
                        pltpu.VMEM((SUBLANES - 1, CONV_HALO + tc, c), F32)],
        compiler_params=_params("parallel", "arbitrary"),
        name="conv_ln_swish",
    )(a, a, w, b, g, beta)


def _softmax_pair_step(s1, s2, vl, vr, m_ref, l_ref, acc_ref):
    alphas, ps = [], []
    for i, s in enumerate((s1, s2)):
        m_prev = m_ref[i]
        m_new = jnp.maximum(m_prev, jnp.max(s, axis=1, keepdims=True))
        alpha = jnp.exp(m_prev - m_new)
        p = jnp.exp(s - jnp.concatenate([m_new] * (s.shape[1] // LANE), axis=1))
        l_ref[i] = alpha * l_ref[i] + jnp.sum(p, axis=1, keepdims=True)
        m_ref[i] = m_new
        alphas.append(alpha)
        ps.append(p.astype(BF16))
    acc_ref[...] = (jnp.concatenate(alphas, axis=1) * acc_ref[...]
                    + _dot(ps[0], vl) + _dot(ps[1], vr))


def _split_values(v):
    lane = lax.broadcasted_iota(jnp.int32, v.shape, 1)
    zero = jnp.zeros_like(v)
    return jnp.where(lane < LANE, v, zero), jnp.where(lane < LANE, zero, v)


def _init_state(m_ref, l_ref, acc_ref):
    m_ref[...] = jnp.full(m_ref.shape, NEG_INF, F32)
    l_ref[...] = jnp.zeros(l_ref.shape, F32)
    acc_ref[...] = jnp.zeros(acc_ref.shape, F32)


def _split_diff_q(q):
    lane = lax.broadcasted_iota(jnp.int32, q.shape, 1)
    qs = q * jnp.asarray(DIFF_HEAD_DIM ** -0.5, q.dtype)
    zero = jnp.zeros_like(qs)
    return jnp.where(lane < DIFF_HEAD_DIM, qs, zero), jnp.where(lane < DIFF_HEAD_DIM, zero, qs)


def _diff_lambda(lq1, lk1, lq2, lk2, lam_init):
    return (jnp.exp(jnp.sum(lq1 * lk1, axis=-1, keepdims=True))
            - jnp.exp(jnp.sum(lq2 * lk2, axis=-1, keepdims=True)) + lam_init)


def _diff_finish(o1, o2, lam, subln, lam_init):
    o = o1 - lam * o2
    return _rms(o, subln) * (1.0 - lam_init)


def _causal_sweep(qi, scores, update):
    npairs = jnp.where(qi > 0, lax.shift_right_logical(qi - 1, 1), 0)
    scores(0, 0)

    def pair(u, carry):
        t = 2 * u
        scores(t + 1, 1)
        update(t, 0, False)
        scores(t + 2, 0)
        update(t + 1, 1, False)
        return carry

    lax.fori_loop(0, npairs, pair, 0)

    t0 = 2 * npairs
    scores(jnp.minimum(t0 + 1, qi), 1)
    update(t0, 0, True)

    @pl.when(t0 + 1 <= qi)
    def _():
        scores(jnp.minimum(t0 + 2, qi), 0)
        update(t0 + 1, 1, True)

    @pl.when(t0 + 2 <= qi)
    def _():
        update(t0 + 2, 0, True)


def _near_tile(t, qi):
    return jnp.clip(t - qi + 2, 0, 2)


def _diff_kernel(q_ref, k_ref, v_ref, km_ref, vm_ref, bn_ref, bm_ref,
                 lq1_ref, lk1_ref, lq2_ref, lk2_ref, sub_ref, li_ref,
                 o_ref, m_ref, l_ref, acc_ref, s_ref, vl_ref, vr_ref):
    qi = pl.program_id(2)
    tq = q_ref.shape[1]
    q1, q2 = _split_diff_q(q_ref[0])
    _init_state(m_ref, l_ref, acc_ref)

    def widen(v):
        return _split_values(jnp.concatenate([v, v], axis=1))

    @pl.when(qi == 0)
    def _():
        vl, vr = widen(v_ref[0])
        vl_ref[...] = vl
        vr_ref[...] = vr

    km = km_ref[0]
    vml, vmr = widen(vm_ref[0])
    bm = bm_ref[0, 0]
    _softmax_pair_step(_dot_nt(q1, km) + bm, _dot_nt(q2, km) + bm, vml, vmr, m_ref, l_ref, acc_ref)

    def scores(t, slot):
        k = k_ref[0, pl.ds(pl.multiple_of(t * tq, tq), tq), :]
        s_ref[slot, 0] = _dot_nt(q1, k)
        s_ref[slot, 1] = _dot_nt(q2, k)

    def update(t, slot, biased):
        rows = pl.ds(pl.multiple_of(t * tq, tq), tq)
        s1 = s_ref[slot, 0]
        s2 = s_ref[slot, 1]
        if biased:
            bias = bn_ref[0, _near_tile(t, qi)]
            s1 = s1 + bias
            s2 = s2 + bias
        _softmax_pair_step(s1, s2, vl_ref[rows, :], vr_ref[rows, :], m_ref, l_ref, acc_ref)

    _causal_sweep(qi, scores, update)

    li = li_ref[...]
    lam = _diff_lambda(lq1_ref[...], lk1_ref[...], lq2_ref[...], lk2_ref[...], li)
    o = _diff_finish(acc_ref[:, :DIFF_V_DIM] / l_ref[0], acc_ref[:, DIFF_V_DIM:] / l_ref[1], lam,
                     sub_ref[...], li)
    o_ref[0] = o.astype(o_ref.dtype)


def _diff_meta_kernel(q_ref, k_ref, v_ref, bm_ref, lq1_ref, lk1_ref, lq2_ref, lk2_ref, sub_ref, li_ref,
                      prev_ref, o_ref):
    del prev_ref
    li = li_ref[...]
    lam = _diff_lambda(lq1_ref[...], lk1_ref[...], lq2_ref[...], lk2_ref[...], li)
    for hd in range(DIFF_HEADS):
        sl = slice(hd * LANE, (hd + 1) * LANE)
        q1, q2 = _split_diff_q(q_ref[0, :, sl])
        k = k_ref[0, :, sl]
        v = v_ref[0, :, sl]
        outs = []
        for qz in (q1, q2):
            s = _dot_nt(qz, k) + bm_ref[hd]
            p = jnp.exp(s - jnp.max(s, axis=1, keepdims=True))
            outs.append(_dot(p.astype(BF16), v) / jnp.sum(p, axis=1, keepdims=True))
        o_ref[0, :, sl] = _diff_finish(outs[0], outs[1], lam, sub_ref[...], li).astype(o_ref.dtype)


def _diff_attention(qkv, bias, lq1, lk1, lq2, lk2, subln, lam_init, seq, tq):
    bsz, lp, _ = qkv.shape
    nq = seq // tq
    mb = seq // META_ROWS
    hq = DIFF_HEADS
    vec = lambda n: pl.BlockSpec((1, n), lambda *_: (0, 0))
    small = [vec(DIFF_HEAD_DIM)] * 4 + [vec(DIFF_V_DIM), vec(1)]
    smalls = (lq1, lk1, lq2, lk2, subln, lam_init)
    out = pl.pallas_call(
        _diff_kernel,
        out_shape=jax.ShapeDtypeStruct((bsz, lp, hq * DIFF_V_DIM), BF16),
        grid=(bsz, hq, nq),
        in_specs=[
            pl.BlockSpec((1, tq, LANE), lambda b, h, i: (b, i, h)),
            pl.BlockSpec((1, seq, LANE), lambda b, h, i: (b, 0, hq + h)),
            pl.BlockSpec((1, seq, LANE), lambda b, h, i: (b, 0, 2 * hq + h)),
            pl.BlockSpec((1, META_ROWS, LANE), lambda b, h, i: (b, mb, hq + h)),
            pl.BlockSpec((1, META_ROWS, LANE), lambda b, h, i: (b, mb, 2 * hq + h)),
            pl.BlockSpec((1, 3, tq, tq), lambda b, h, i: (h, 0, 0, 0)),
            pl.BlockSpec((1, 1, tq, META_ROWS), lambda b, h, i: (h, jnp.minimum(i, 1), 0, 0)),
        ] + small,
        out_specs=pl.BlockSpec((1, tq, LANE), lambda b, h, i: (b, i, h)),
        scratch_shapes=[pltpu.VMEM((2, tq, LANE), F32), pltpu.VMEM((2, tq, LANE), F32),
                        pltpu.VMEM((tq, 2 * DIFF_V_DIM), F32), pltpu.VMEM((2, 2, tq, tq), F32),
                        pltpu.VMEM((seq, 2 * DIFF_V_DIM), BF16), pltpu.VMEM((seq, 2 * DIFF_V_DIM), BF16)],
        compiler_params=_params("parallel", "parallel", "arbitrary"),
        name="diff_attention",
    )(qkv, qkv, qkv, qkv, qkv, bias["near"], bias["meta_keys"], *smalls)

    wide = hq * LANE
    return pl.pallas_call(
        _diff_meta_kernel,
        out_shape=jax.ShapeDtypeStruct(out.shape, out.dtype),
        grid=(bsz,),
        in_specs=[
            pl.BlockSpec((1, META_ROWS, wide), lambda b: (b, mb, 0)),
            pl.BlockSpec((1, META_ROWS, wide), lambda b: (b, mb, 1)),
            pl.BlockSpec((1, META_ROWS, wide), lambda b: (b, mb, 2)),
            pl.BlockSpec(bias["meta_meta"].shape, lambda b: (0, 0, 0)),
        ] + small + [pl.BlockSpec(memory_space=pl.ANY)],
        out_specs=pl.BlockSpec((1, META_ROWS, wide), lambda b: (b, mb, 0)),
        input_output_aliases={10: 0},
        compiler_params=_params("parallel"),
        name="diff_attention_meta",
    )(qkv, qkv, qkv, bias["meta_meta"], *smalls, out)


def _mla_kernel(q_ref, k_ref, v_ref, km_ref, vm_ref, md_ref, mm_ref, o_ref, m_ref, l_ref, acc_ref, s_ref,
                vl_ref, vr_ref, *, scale):
    qi = pl.program_id(2)
    tq = q_ref.shape[1]
    qa = q_ref[0, :, :MLA_QK_PAD]
    qb = q_ref[0, :, MLA_QK_PAD:]
    _init_state(m_ref, l_ref, acc_ref)

    @pl.when(qi == 0)
    def _():
        vl, vr = _split_values(v_ref[0])
        vl_ref[...] = vl
        vr_ref[...] = vr

    km = km_ref[0]
    vml, vmr = _split_values(vm_ref[0])
    mm = mm_ref[...]
    _softmax_pair_step(_dot_nt(qa, km[:, :MLA_QK_PAD]) * scale + mm,
                       _dot_nt(qb, km[:, MLA_QK_PAD:]) * scale + mm, vml, vmr, m_ref, l_ref, acc_ref)

    def scores(t, slot):
        rows = pl.ds(pl.multiple_of(t * tq, tq), tq)
        s_ref[slot, 0] = _dot_nt(qa, k_ref[0, rows, :MLA_QK_PAD]) * scale
        s_ref[slot, 1] = _dot_nt(qb, k_ref[0, rows, MLA_QK_PAD:]) * scale

    def update(t, slot, masked):
        rows = pl.ds(pl.multiple_of(t * tq, tq), tq)
        s1 = s_ref[slot, 0]
        s2 = s_ref[slot, 1]
        if masked:
            mask = md_ref[_near_tile(t, qi)]
            s1 = s1 + mask
            s2 = s2 + mask
        _softmax_pair_step(s1, s2, vl_ref[rows, :], vr_ref[rows, :], m_ref, l_ref, acc_ref)

    _causal_sweep(qi, scores, update)
    o_ref[0] = (acc_ref[...] / jnp.concatenate([l_ref[0], l_ref[1]], axis=1)).astype(o_ref.dtype)


def _mla_meta_kernel(q_ref, k_ref, v_ref, mm_ref, prev_ref, o_ref, *, scale):
    del prev_ref
    for hd in range(MLA_HEADS):
        qk = slice(hd * MLA_QK_PAD, (hd + 1) * MLA_QK_PAD)
        vs = slice(hd * MLA_V_DIM, (hd + 1) * MLA_V_DIM)
        s = _dot_nt(q_ref[0, :, qk], k_ref[0, :, qk]) * scale + mm_ref[...]
        p = jnp.exp(s - jnp.max(s, axis=1, keepdims=True))
        o = _dot(p.astype(BF16), v_ref[0, :, vs]) / jnp.sum(p, axis=1, keepdims=True)
        o_ref[0, :, vs] = o.astype(o_ref.dtype)


def _mla_attention(qm, km, vm, masks, seq, tq):
    bsz, lp, _ = qm.shape
    nq = seq // tq
    mb = seq // META_ROWS
    scale = (MLA_NOPE_DIM + MLA_ROPE_DIM) ** -0.5
    out = pl.pallas_call(
        functools.partial(_mla_kernel, scale=scale),
        out_shape=jax.ShapeDtypeStruct((bsz, lp, MLA_HEADS * MLA_V_DIM), BF16),
        grid=(bsz, MLA_HEADS // 2, nq),
        in_specs=[
            pl.BlockSpec((1, tq, 2 * MLA_QK_PAD), lambda b, h, i: (b, i, h)),
            pl.BlockSpec((1, seq, 2 * MLA_QK_PAD), lambda b, h, i: (b, 0, h)),
            pl.BlockSpec((1, seq, 2 * MLA_V_DIM), lambda b, h, i: (b, 0, h)),
            pl.BlockSpec((1, META_ROWS, 2 * MLA_QK_PAD), lambda b, h, i: (b, mb, h)),
            pl.BlockSpec((1, META_ROWS, 2 * MLA_V_DIM), lambda b, h, i: (b, mb, h)),
            pl.BlockSpec((3, tq, tq), lambda b, h, i: (0, 0, 0)),
            pl.BlockSpec((1, META_ROWS), lambda b, h, i: (0, 0)),
        ],
        out_specs=pl.BlockSpec((1, tq, 2 * MLA_V_DIM), lambda b, h, i: (b, i, h)),
        scratch_shapes=[pltpu.VMEM((2, tq, LANE), F32), pltpu.VMEM((2, tq, LANE), F32),
                        pltpu.VMEM((tq, 2 * MLA_V_DIM), F32), pltpu.VMEM((2, 2, tq, tq), F32),
                        pltpu.VMEM((seq, 2 * MLA_V_DIM), BF16), pltpu.VMEM((seq, 2 * MLA_V_DIM), BF16)],
        compiler_params=_params("parallel", "parallel", "arbitrary"),
        name="mla_attention",
    )(qm, km, vm, km, vm, masks["near"], masks["meta_cols"])

    return pl.pallas_call(
        functools.partial(_mla_meta_kernel, scale=scale),
        out_shape=jax.ShapeDtypeStruct(out.shape, out.dtype),
        grid=(bsz,),
        in_specs=[
            pl.BlockSpec((1, META_ROWS, MLA_HEADS * MLA_QK_PAD), lambda b: (b, mb, 0)),
            pl.BlockSpec((1, META_ROWS, MLA_HEADS * MLA_QK_PAD), lambda b: (b, mb, 0)),
            pl.BlockSpec((1, META_ROWS, MLA_HEADS * MLA_V_DIM), lambda b: (b, mb, 0)),
            pl.BlockSpec((1, META_ROWS), lambda b: (0, 0)),
            pl.BlockSpec(memory_space=pl.ANY),
        ],
        out_specs=pl.BlockSpec((1, META_ROWS, MLA_HEADS * MLA_V_DIM), lambda b: (b, mb, 0)),
        input_output_aliases={4: 0},
        compiler_params=_params("parallel"),
        name="mla_attention_meta",
    )(qm, km, vm, masks["meta_cols"], out)


def _mix_kernel(h_ref, g_ref, xa_ref, xb_ref, xc_ref, ga_ref, gb_ref, gc_ref, wa_ref, wb_ref, wc_ref,
                o_ref, hn_ref):
    @pl.when(pl.program_id(1) == 0)
    def _():
        hn_ref[...] = _rms(h_ref[...], g_ref[...]).astype(BF16)

    hn = hn_ref[...]
    mixed = jax.nn.sigmoid(_dot(hn, ga_ref[...])) * _dot(xa_ref[...], wa_ref[...])
    mixed = mixed + jax.nn.sigmoid(_dot(hn, gb_ref[...])) * _dot(xb_ref[...], wb_ref[...])
    mixed = mixed + jax.nn.sigmoid(_dot(hn, gc_ref[...])) * _dot(xc_ref[...], wc_ref[...])
    o_ref[...] = mixed.astype(o_ref.dtype)


def _gated_mix(h, g, xa, xb, xc, w_gates, wa, wb, wc, tm, tn):
    t, d = h.shape
    nj = d // tn
    branch = lambda x: pl.BlockSpec((tm, x.shape[1]), lambda i, j: (i, 0))
    gate = lambda n: pl.BlockSpec((d, tn), lambda i, j: (0, n * nj + j))
    wbr = lambda w: pl.BlockSpec((w.shape[0], tn), lambda i, j: (0, j))
    return pl.pallas_call(
        _mix_kernel,
        out_shape=jax.ShapeDtypeStruct((t, d), BF16),
        grid=(t // tm, nj),
        in_specs=[
            pl.BlockSpec((tm, d), lambda i, j: (i, 0)),
            pl.BlockSpec((1, d), lambda i, j: (0, 0)),
            branch(xa), branch(xb), branch(xc),
            gate(0), gate(1), gate(2),
            wbr(wa), wbr(wb), wbr(wc),
        ],
        out_specs=pl.BlockSpec((tm, tn), lambda i, j: (i, j)),
        scratch_shapes=[pltpu.VMEM((tm, d), BF16)],
        compiler_params=_params("parallel", "arbitrary"),
        name="gated_mix",
    )(h, g, xa, xb, xc, w_gates, w_gates, w_gates, wa, wb, wc)


def _residual_mm_kernel(h_ref, x_ref, w_ref, o_ref):
    o_ref[...] = h_ref[...] + _dot(x_ref[...], w_ref[...])


def _residual_matmul(h, x, w, tm, tn):
    t, d = h.shape
    k = x.shape[1]
    return pl.pallas_call(
        _residual_mm_kernel,
        out_shape=jax.ShapeDtypeStruct((t, d), F32),
        grid=(t // tm, d // tn),
        in_specs=[
            pl.BlockSpec((tm, tn), lambda i, j: (i, j)),
            pl.BlockSpec((tm, k), lambda i, j: (i, 0)),
            pl.BlockSpec((k, tn), lambda i, j: (0, j)),
        ],
        out_specs=pl.BlockSpec((tm, tn), lambda i, j: (i, j)),
        compiler_params=_params("parallel", "arbitrary"),
        name="out_proj",
    )(h, x, w)


def _ffn_kernel(h_ref, g_ref, wg_ref, wu_ref, wd_ref, o_ref, hn_ref):
    f = pl.program_id(1)

    @pl.when(f == 0)
    def _():
        hn_ref[...] = _rms(h_ref[...], g_ref[...]).astype(BF16)

    hn = hn_ref[...]
    gate = _dot(hn, wg_ref[...])
    act = (gate * jax.nn.sigmoid(gate) * _dot(hn, wu_ref[...])).astype(BF16)
    part = _dot(act, wd_ref[...])

    @pl.when(f == 0)
    def _():
        o_ref[...] = h_ref[...] + part

    @pl.when(f > 0)
    def _():
        o_ref[...] += part


def _ffn(h, g, wg, wu, wd, tm, tf):
    t, d = h.shape
    dff = wg.shape[1]
    return pl.pallas_call(
        _ffn_kernel,
        out_shape=jax.ShapeDtypeStruct((t, d), F32),
        grid=(t // tm, dff // tf),
        in_specs=[
            pl.BlockSpec((tm, d), lambda i, f: (i, 0)),
            pl.BlockSpec((1, d), lambda i, f: (0, 0)),
            pl.BlockSpec((d, tf), lambda i, f: (0, f)),
            pl.BlockSpec((d, tf), lambda i, f: (0, f)),
            pl.BlockSpec((tf, d), lambda i, f: (f, 0)),
        ],
        out_specs=pl.BlockSpec((tm, d), lambda i, f: (i, 0)),
        scratch_shapes=[pltpu.VMEM((tm, d), BF16)],
        compiler_params=_params("parallel", "arbitrary"),
        name="swiglu_ffn",
    )(h, g, wg, wu, wd)


def _final_norm_kernel(h_ref, g_ref, o_ref):
    o_ref[0] = _rms(h_ref[0], g_ref[...])


def _final_norm(h3, g, seq, tr):
    bsz, _, d = h3.shape
    return pl.pallas_call(
        _final_norm_kernel,
        out_shape=jax.ShapeDtypeStruct((bsz, seq, d), F32),
        grid=(bsz, seq // tr),
        in_specs=[
            pl.BlockSpec((1, tr, d), lambda b, i: (b, i, 0)),
            pl.BlockSpec((1, d), lambda b, i: (0, 0)),
        ],
        out_specs=pl.BlockSpec((1, tr, d), lambda b, i: (b, i, 0)),
        compiler_params=_params("parallel", "parallel"),
        name="final_norm",
    )(h3, g)


def _t5_bucket(rel):
    nb = REL_BUCKETS // 2
    ret = jnp.where(rel > 0, nb, 0)
    n = jnp.abs(rel)
    max_exact = nb // 2
    nf = jnp.maximum(n, 1).astype(F32)
    large = max_exact + (jnp.log(nf / max_exact) / math.log(REL_MAX_DIST / max_exact)
                         * (nb - max_exact)).astype(jnp.int32)
    large = jnp.minimum(large, nb - 1)
    return ret + jnp.where(n < max_exact, n, large)


def _position_tables(rel_bias, seq, tq):
    far = rel_bias[_t5_bucket(jnp.asarray(-REL_MAX_DIST, jnp.int32))]
    table = (rel_bias - far[None, :]).astype(F32)

    def bias(rel, visible):
        bucket = _t5_bucket(rel)[None]
        t = jnp.zeros((DIFF_HEADS,) + rel.shape, F32)
        for b in range(REL_BUCKETS):
            t = t + jnp.where(bucket == b, table[b][:, None, None], 0.0)
        return jnp.where(visible[None], t, NEG_INF)

    r = jnp.arange(tq, dtype=jnp.int32)[:, None]
    c = jnp.arange(tq, dtype=jnp.int32)[None, :]
    chunk_vis = (c // CHUNK) <= (r // CHUNK)
    all_vis = jnp.ones((tq, tq), bool)
    m = jnp.arange(META_ROWS, dtype=jnp.int32)
    meta_vis = jnp.broadcast_to((m < N_META)[None, :], (tq, META_ROWS))
    diff = {
        "near": jnp.stack([jnp.zeros((DIFF_HEADS, tq, tq), F32), bias(c - tq - r, all_vis),
                           bias(c - r, chunk_vis)], axis=1),
        "meta_keys": jnp.stack([bias(m[None, :] - N_META - r, meta_vis),
                                bias(jnp.full((tq, META_ROWS), -REL_MAX_DIST - tq, jnp.int32), meta_vis)],
                               axis=1),
        "meta_meta": bias(m[None, :] - m[:, None],
                          jnp.broadcast_to((m < N_META)[None, :], (META_ROWS, META_ROWS))),
    }
    mla = {
        "near": jnp.stack([jnp.zeros((tq, tq), F32), jnp.zeros((tq, tq), F32),
                           jnp.where(chunk_vis, 0.0, NEG_INF).astype(F32)]),
        "meta_cols": jnp.where(m < N_META, 0.0, NEG_INF).astype(F32)[None, :],
    }
    return diff, mla


def _rope_tables(seq):
    half = MLA_ROPE_DIM // 2
    row = jnp.arange(seq + META_ROWS, dtype=jnp.int32)
    pos = jnp.where(row < seq, row + N_META, row - seq)
    inv = ROPE_THETA ** (-jnp.arange(half, dtype=F32) / half)
    ang = pos.astype(F32)[:, None] * inv[None, :]
    cos, sin = jnp.cos(ang), jnp.sin(ang)
    pad = jnp.zeros((seq + META_ROWS, LANE - MLA_ROPE_DIM), F32)
    return (jnp.concatenate([cos, cos, pad], axis=1), jnp.concatenate([-sin, sin, pad], axis=1))


def _swap_halves(w):
    half = w.shape[-1] // 2
    return jnp.concatenate([w[..., half:], w[..., :half]], axis=-1)


def _lane_pad(w):
    return jnp.concatenate([w, jnp.zeros(w.shape[:-1] + (LANE - w.shape[-1],), w.dtype)], axis=-1)


def kernel(x, meta_tokens, rel_bias, norm_mix, w_in, conv_w, conv_b, conv_ln_g, conv_ln_b, w_br_conv,
           diff_lq1, diff_lk1, diff_lq2, diff_lk2, diff_subln, w_br_diff, mla_q_norm, w_uq, mla_kv_norm,
           w_ukv, w_br_mla, w_out, norm_ffn, w_ffn_gate, w_ffn_up, w_ffn_down, final_norm):
    bsz, seq, d = x.shape
    depth = w_in.shape[0]
    dff = w_ffn_gate.shape[-1]
    lp = seq + META_ROWS
    t = bsz * lp
    assert seq % META_ROWS == 0 and d % LANE == 0

    tq = _tile(seq, 512)
    tm = _tile(lp, 640)
    tn = _tile(d, 1024)
    tf = _tile(dff, 512)
    tn_mix = _tile(d, 512)
    tn_out = _tile(d, 2048)

    w_in_b = w_in.astype(BF16)
    w_val = w_in_b[:, :, OFF_A:OFF_A + CONV_WIDTH]
    w_gate = w_in_b[:, :, OFF_A + CONV_WIDTH:OFF_BQ]
    w_qkv = w_in_b[:, :, OFF_BQ:OFF_CQ]
    w_kr = w_in_b[:, :, OFF_CR:OFF_G]
    w_c = jnp.concatenate([w_in_b[:, :, OFF_CQ:OFF_CR], _lane_pad(w_kr), _lane_pad(_swap_halves(w_kr))], axis=-1)
    w_gates = w_in_b[:, :, OFF_G:]
    uq = w_uq.astype(BF16).reshape(depth, MLA_Q_RANK, MLA_HEADS, MLA_NOPE_DIM + MLA_ROPE_DIM)
    uq_rope = uq[..., MLA_NOPE_DIM:]
    w_uq_p = jnp.concatenate([uq[..., :MLA_NOPE_DIM], _lane_pad(uq_rope), _lane_pad(_swap_halves(uq_rope))],
                             axis=-1).reshape(depth, MLA_Q_RANK, MLA_HEADS * MLA_UQ_PAD)
    w_ukv_b = w_ukv.astype(BF16)
    w_bra, w_brb, w_brc = w_br_conv.astype(BF16), w_br_diff.astype(BF16), w_br_mla.astype(BF16)
    w_out_b = w_out.astype(BF16)
    w_fg, w_fu, w_fd = w_ffn_gate.astype(BF16), w_ffn_up.astype(BF16), w_ffn_down.astype(BF16)

    diff_tabs, mla_tabs = _position_tables(rel_bias, seq, tq)
    cos_t, sin_t = _rope_tables(seq)

    row = lambda v: v.reshape(1, -1)
    h = jnp.concatenate([x, jnp.broadcast_to(meta_tokens[None].astype(x.dtype), (bsz, N_META, d)),
                         jnp.zeros((bsz, META_ROWS - N_META, d), x.dtype)], axis=1).reshape(t, d)

    for l in range(depth):
        lam_init = jnp.full((1, 1), 0.8 - 0.6 * math.exp(-0.3 * l), F32)
        g_mix = row(norm_mix[l])
        a = _norm_glu(h, g_mix, w_val[l], w_gate[l], tm, tn)
        qkv = _norm_matmul(h, g_mix, w_qkv[l], BF16, tm, tn)
        qm, km, vm = _mla_proj(h, g_mix, w_c[l], row(mla_q_norm[l]), row(mla_kv_norm[l]), w_uq_p[l],
                               w_ukv_b[l], cos_t, sin_t, tm)
        ya = _conv_branch(a.reshape(bsz, lp, CONV_WIDTH), conv_w[l], row(conv_b[l]), row(conv_ln_g[l]),
                          row(conv_ln_b[l]), seq)
        yb = _diff_attention(qkv.reshape(bsz, lp, -1), diff_tabs, row(diff_lq1[l]), row(diff_lk1[l]),
                             row(diff_lq2[l]), row(diff_lk2[l]), row(diff_subln[l]), lam_init, seq, tq)
        yc = _mla_attention(qm.reshape(bsz, lp, -1), km.reshape(bsz, lp, -1), vm.reshape(bsz, lp, -1),
                            mla_tabs, seq, tq)
        mixed = _gated_mix(h, g_mix, ya.reshape(t, -1), yb.reshape(t, -1), yc.reshape(t, -1),
                           w_gates[l], w_bra[l], w_brb[l], w_brc[l], tm, tn_mix)
        h = _residual_matmul(h, mixed, w_out_b[l], tm, tn_out)
        h = _ffn(h, row(norm_ffn[l]), w_fg[l], w_fu[l], w_fd[l], tm, tf)

    return _final_norm(h.reshape(bsz, lp, d), row(final_norm), seq, _tile(seq, 512))
```

```python
import functools
import math

import jax
import jax.numpy as jnp
from jax import lax
from jax.experimental import pallas as pl
from jax.experimental.pallas import tpu as pltpu

F32 = jnp.float32
BF16 = jnp.bfloat16

CHUNK = 64
N_META = 16
NORM_EPS = 1e-6
NEG_INF = -1e30

CONV_WIDTH = 1024
CONV_KERNEL = 31
DIFF_HEADS = 8
DIFF_HEAD_DIM = 64
DIFF_V_DIM = 2 * DIFF_HEAD_DIM
REL_BUCKETS = 32
REL_MAX_DIST = 128
MLA_HEADS = 8
MLA_Q_RANK = 512
MLA_KV_RANK = 256
MLA_NOPE_DIM = 128
MLA_ROPE_DIM = 64
MLA_V_DIM = 128
ROPE_THETA = 10000.0

DIFF_QK = DIFF_HEADS * 2 * DIFF_HEAD_DIM
OFF_A = 0
OFF_BQ = OFF_A + 2 * CONV_WIDTH
OFF_BK = OFF_BQ + DIFF_QK
OFF_BV = OFF_BK + DIFF_QK
OFF_CQ = OFF_BV + DIFF_HEADS * DIFF_V_DIM
OFF_CKV = OFF_CQ + MLA_Q_RANK
OFF_CR = OFF_CKV + MLA_KV_RANK
OFF_G = OFF_CR + MLA_ROPE_DIM

LANE = 128
META_ROWS = 128
MLA_QK_PAD = 2 * LANE
MLA_UQ_PAD = 3 * LANE
MLA_SCALE = (MLA_NOPE_DIM + MLA_ROPE_DIM) ** -0.5
VMEM_LIMIT = 56 * 1024 * 1024


def _params(*sem):
    return pltpu.CompilerParams(dimension_semantics=sem, vmem_limit_bytes=VMEM_LIMIT)


def _tile(n, cap, unit=LANE):
    best = None
    for t in range(unit, min(n, cap) + 1, unit):
        if n % t == 0:
            best = t
    assert best is not None, (n, cap, unit)
    return best


def _rms(x, g):
    ms = jnp.mean(x * x, axis=-1, keepdims=True)
    return x * lax.rsqrt(ms + NORM_EPS) * g


def _dot(a, b):
    return jnp.dot(a, b, preferred_element_type=F32)


def _dot_nt(a, b):
    return lax.dot_general(a, b, (((1,), (1,)), ((), ())), preferred_element_type=F32)


def _norm_mm_kernel(h_ref, g_ref, w_ref, o_ref, hn_ref):
    @pl.when(pl.program_id(1) == 0)
    def _():
        hn_ref[...] = _rms(h_ref[...], g_ref[...]).astype(BF16)

    o_ref[...] = _dot(hn_ref[...], w_ref[...]).astype(o_ref.dtype)


def _norm_matmul(h, g, w, out_dtype, tm, tn):
    t, d = h.shape
    n = w.shape[1]
    return pl.pallas_call(
        _norm_mm_kernel,
        out_shape=jax.ShapeDtypeStruct((t, n), out_dtype),
        grid=(t // tm, n // tn),
        in_specs=[
            pl.BlockSpec((tm, d), lambda i, j: (i, 0)),
            pl.BlockSpec((1, d), lambda i, j: (0, 0)),
            pl.BlockSpec((d, tn), lambda i, j: (0, j)),
        ],
        out_specs=pl.BlockSpec((tm, tn), lambda i, j: (i, j)),
        scratch_shapes=[pltpu.VMEM((tm, d), BF16)],
        compiler_params=_params("parallel", "arbitrary"),
        name="norm_qkv",
    )(h, g, w)


def _glu_kernel(h_ref, g_ref, wv_ref, wg_ref, o_ref, hn_ref):
    @pl.when(pl.program_id(1) == 0)
    def _():
        hn_ref[...] = _rms(h_ref[...], g_ref[...]).astype(BF16)

    hn = hn_ref[...]
    o_ref[...] = _dot(hn, wv_ref[...]) * jax.nn.sigmoid(_dot(hn, wg_ref[...]))


def _norm_glu(h, g, w_val, w_gate, tm, tn):
    t, d = h.shape
    n = w_val.shape[1]
    return pl.pallas_call(
        _glu_kernel,
        out_shape=jax.ShapeDtypeStruct((t, n), F32),
        grid=(t // tm, n // tn),
        in_specs=[
            pl.BlockSpec((tm, d), lambda i, j: (i, 0)),
            pl.BlockSpec((1, d), lambda i, j: (0, 0)),
            pl.BlockSpec((d, tn), lambda i, j: (0, j)),
            pl.BlockSpec((d, tn), lambda i, j: (0, j)),
        ],
        out_specs=pl.BlockSpec((tm, tn), lambda i, j: (i, j)),
        scratch_shapes=[pltpu.VMEM((tm, d), BF16)],
        compiler_params=_params("parallel", "arbitrary"),
        name="norm_glu",
    )(h, g, w_val, w_gate)


def _mla_proj_kernel(h_ref, g_ref, wc_ref, gq_ref, gkv_ref, wuq_ref, wukv_ref, cos_ref, sin_ref,
                     q_ref, k_ref, v_ref):
    hn = _rms(h_ref[...], g_ref[...]).astype(BF16)
    c = _dot(hn, wc_ref[...])
    cos = cos_ref[...]
    sin = sin_ref[...]
    o_kr = MLA_Q_RANK + MLA_KV_RANK
    kr = (c[:, o_kr:o_kr + LANE] * cos + c[:, o_kr + LANE:o_kr + 2 * LANE] * sin).astype(BF16)
    cqn = _rms(c[:, :MLA_Q_RANK], gq_ref[...]).astype(BF16)
    ckvn = _rms(c[:, MLA_Q_RANK:o_kr], gkv_ref[...]).astype(BF16)
    for hd in range(MLA_HEADS):
        q = _dot(cqn, wuq_ref[:, hd * MLA_UQ_PAD:(hd + 1) * MLA_UQ_PAD]) * MLA_SCALE
        kv = _dot(ckvn, wukv_ref[:, hd * 2 * LANE:(hd + 1) * 2 * LANE])
        qo = hd * MLA_QK_PAD
        q_ref[:, qo:qo + LANE] = q[:, :LANE].astype(BF16)
        q_ref[:, qo + LANE:qo + 2 * LANE] = (q[:, LANE:2 * LANE] * cos + q[:, 2 * LANE:] * sin).astype(BF16)
        k_ref[:, qo:qo + LANE] = kv[:, :LANE].astype(BF16)
        k_ref[:, qo + LANE:qo + 2 * LANE] = kr
        v_ref[:, hd * LANE:(hd + 1) * LANE] = kv[:, LANE:].astype(BF16)


def _mla_proj(h, g, wc, gq, gkv, wuq, wukv, cos_t, sin_t, tm):
    t, d = h.shape
    lp = cos_t.shape[0]
    nper = lp // tm
    const = lambda i: (0, 0)
    return pl.pallas_call(
        _mla_proj_kernel,
        out_shape=(
            jax.ShapeDtypeStruct((t, MLA_HEADS * MLA_QK_PAD), BF16),
            jax.ShapeDtypeStruct((t, MLA_HEADS * MLA_QK_PAD), BF16),
            jax.ShapeDtypeStruct((t, MLA_HEADS * MLA_V_DIM), BF16),
        ),
        grid=(t // tm,),
        in_specs=[
            pl.BlockSpec((tm, d), lambda i: (i, 0)),
            pl.BlockSpec((1, d), const),
            pl.BlockSpec(wc.shape, const),
            pl.BlockSpec(gq.shape, const),
            pl.BlockSpec(gkv.shape, const),
            pl.BlockSpec(wuq.shape, const),
            pl.BlockSpec(wukv.shape, const),
            pl.BlockSpec((tm, LANE), lambda i: (i % nper, 0)),
            pl.BlockSpec((tm, LANE), lambda i: (i % nper, 0)),
        ],
        out_specs=(
            pl.BlockSpec((tm, MLA_HEADS * MLA_QK_PAD), lambda i: (i, 0)),
            pl.BlockSpec((tm, MLA_HEADS * MLA_QK_PAD), lambda i: (i, 0)),
            pl.BlockSpec((tm, MLA_HEADS * MLA_V_DIM), lambda i: (i, 0)),
        ),
        compiler_params=_params("parallel"),
        name="mla_proj",
    )(h, g, wc, gq, gkv, wuq, wukv, cos_t, sin_t)


CONV_HALO = 32
CONV_ROWS = 32
SUBLANES = 8


def _conv_kernel(cur_ref, halo_ref, w_ref, b_ref, g_ref, beta_ref, o_ref, xs_ref, xr_ref, *, n_real_tiles):
    t = pl.program_id(1)
    tc = cur_ref.shape[1]
    c = cur_ref.shape[2]

    @pl.when(t == n_real_tiles)
    def _():
        xs_ref[0:CONV_HALO, :] = jnp.zeros((CONV_HALO, c), F32)

    @pl.when(t == 0)
    def _():
        xs_ref[0:CONV_HALO - N_META, :] = jnp.zeros((CONV_HALO - N_META, c), F32)
        xs_ref[CONV_HALO - N_META:CONV_HALO, :] = halo_ref[0, 0:N_META, :]

    @pl.when((t > 0) & (t < n_real_tiles))
    def _():
        xs_ref[0:CONV_HALO, :] = halo_ref[0]

    xs_ref[CONV_HALO:, :] = cur_ref[0]

    n_shift = CONV_HALO + tc - SUBLANES
    for r in range(1, SUBLANES):
        xr_ref[r - 1, 0:n_shift, :] = xs_ref[r:r + n_shift, :]

    first_tap = CONV_HALO - (CONV_KERNEL - 1)
    for rc in range(tc // CONV_ROWS):
        base = rc * CONV_ROWS
        nt = CONV_ROWS // SUBLANES
        acc = jnp.broadcast_to(b_ref[...][None], (nt, SUBLANES, c))
        for k in range(CONV_KERNEL):
            shift = (first_tap + k) % SUBLANES
            lo = base + (first_tap + k) - shift
            if shift == 0:
                x = xs_ref[lo:lo + CONV_ROWS, :]
            else:
                x = xr_ref[shift - 1, lo:lo + CONV_ROWS, :]
            acc = acc + w_ref[k][None] * x.reshape(nt, SUBLANES, c)
        acc = acc.reshape(CONV_ROWS, c)
        mu = jnp.mean(acc, axis=-1, keepdims=True)
        ctr = acc - mu
        var = jnp.mean(ctr * ctr, axis=-1, keepdims=True)
        y = ctr * lax.rsqrt(var + NORM_EPS) * g_ref[...] + beta_ref[...]
        o_ref[0, base:base + CONV_ROWS, :] = (y * jax.nn.sigmoid(y)).astype(o_ref.dtype)


def _conv_branch(a, w, b, g, beta, seq):
    bsz, lp, c = a.shape
    tc = META_ROWS
    n_real = seq // tc
    hpt = tc // CONV_HALO

    def halo_map(bi, t):
        inside = (t > 0) & (t < n_real)
        return (bi, jnp.where(inside, t * hpt - 1, seq // CONV_HALO), 0)

    const = lambda bi, t: (0, 0)
    return pl.pallas_call(
        functools.partial(_conv_kernel, n_real_tiles=n_real),
        out_shape=jax.ShapeDtypeStruct((bsz, lp, c), BF16),
        grid=(bsz, n_real + 1),
        in_specs=[
            pl.BlockSpec((1, tc, c), lambda bi, t: (bi, t, 0)),
            pl.BlockSpec((1, CONV_HALO, c), halo_map),
            pl.BlockSpec((CONV_KERNEL, SUBLANES, c), lambda bi, t: (0, 0, 0)),
            pl.BlockSpec((SUBLANES, c), const),
            pl.BlockSpec((1, c), const),
            pl.BlockSpec((1, c), const),
        ],
        out_specs=pl.BlockSpec((1, tc, c), lambda bi, t: (bi, t, 0)),
        scratch_shapes=[pltpu.VMEM((CONV_HALO + tc, c), F32),
                        pltpu.VMEM((SUBLANES - 1, CONV_HALO + tc, c), F32)],
        compiler_params=_params("parallel", "arbitrary"),
        name="conv_ln_swish",
    )(a, a, jnp.broadcast_to(w[:, None, :], (CONV_KERNEL, SUBLANES, c)), jnp.broadcast_to(b, (SUBLANES, c)),
      g, beta)


def _softmax_pair_step(s1, s2, vl, vr, m_ref, l_ref, acc_ref):
    alphas, ps = [], []
    for i, s in enumerate((s1, s2)):
        m_prev = m_ref[i]
        m_new = jnp.maximum(m_prev, jnp.max(s, axis=1, keepdims=True))
        alpha = jnp.exp(m_prev - m_new)
        p = jnp.exp(s - jnp.concatenate([m_new] * (s.shape[1] // LANE), axis=1))
        l_ref[i] = alpha * l_ref[i] + jnp.sum(p, axis=1, keepdims=True)
        m_ref[i] = m_new
        alphas.append(alpha)
        ps.append(p.astype(BF16))
    acc_ref[...] = (jnp.concatenate(alphas, axis=1) * acc_ref[...]
                    + _dot(ps[0], vl) + _dot(ps[1], vr))


def _split_values(v):
    lane = lax.broadcasted_iota(jnp.int32, v.shape, 1)
    zero = jnp.zeros_like(v)
    return jnp.where(lane < LANE, v, zero), jnp.where(lane < LANE, zero, v)


def _init_state(m_ref, l_ref, acc_ref):
    m_ref[...] = jnp.full(m_ref.shape, NEG_INF, F32)
    l_ref[...] = jnp.zeros(l_ref.shape, F32)
    acc_ref[...] = jnp.zeros(acc_ref.shape, F32)


def _split_diff_q(q):
    lane = lax.broadcasted_iota(jnp.int32, q.shape, 1)
    qs = q * jnp.asarray(DIFF_HEAD_DIM ** -0.5, q.dtype)
    zero = jnp.zeros_like(qs)
    return jnp.where(lane < DIFF_HEAD_DIM, qs, zero), jnp.where(lane < DIFF_HEAD_DIM, zero, qs)


def _diff_lambda(lq1, lk1, lq2, lk2, lam_init):
    return (jnp.exp(jnp.sum(lq1 * lk1, axis=-1, keepdims=True))
            - jnp.exp(jnp.sum(lq2 * lk2, axis=-1, keepdims=True)) + lam_init)


def _diff_finish(o1, o2, lam, subln, lam_init):
    o = o1 - lam * o2
    return _rms(o, subln) * (1.0 - lam_init)


def _causal_sweep(qi, scores, update):
    npairs = jnp.where(qi > 0, lax.shift_right_logical(qi - 1, 1), 0)
    scores(0, 0)

    def pair(u, carry):
        t = 2 * u
        scores(t + 1, 1)
        update(t, 0, False)
        scores(t + 2, 0)
        update(t + 1, 1, False)
        return carry

    lax.fori_loop(0, npairs, pair, 0)

    t0 = 2 * npairs
    scores(jnp.minimum(t0 + 1, qi), 1)
    update(t0, 0, True)

    @pl.when(t0 + 1 <= qi)
    def _():
        scores(jnp.minimum(t0 + 2, qi), 0)
        update(t0 + 1, 1, True)

    @pl.when(t0 + 2 <= qi)
    def _():
        update(t0 + 2, 0, True)


def _near_tile(t, qi):
    return jnp.clip(t - qi + 2, 0, 2)


def _diff_kernel(q_ref, k_ref, v_ref, km_ref, vm_ref, bn_ref, bm_ref,
                 lq1_ref, lk1_ref, lq2_ref, lk2_ref, sub_ref, li_ref, base_ref,
                 o_ref, m_ref, l_ref, acc_ref, s_ref, vl_ref, vr_ref):
    del base_ref
    qi = pl.program_id(2)
    tq = q_ref.shape[1]
    q1, q2 = _split_diff_q(q_ref[0])
    _init_state(m_ref, l_ref, acc_ref)

    def widen(v):
        return _split_values(jnp.concatenate([v, v], axis=1))

    @pl.when(qi == 0)
    def _():
        vl, vr = widen(v_ref[0])
        vl_ref[...] = vl
        vr_ref[...] = vr

    km = km_ref[0]
    vml, vmr = widen(vm_ref[0])
    bm = bm_ref[0, 0]
    _softmax_pair_step(_dot_nt(q1, km) + bm, _dot_nt(q2, km) + bm, vml, vmr, m_ref, l_ref, acc_ref)

    def scores(t, slot):
        k = k_ref[0, pl.ds(pl.multiple_of(t * tq, tq), tq), :]
        s_ref[slot, 0] = _dot_nt(q1, k)
        s_ref[slot, 1] = _dot_nt(q2, k)

    def update(t, slot, biased):
        rows = pl.ds(pl.multiple_of(t * tq, tq), tq)
        s1 = s_ref[slot, 0]
        s2 = s_ref[slot, 1]
        if biased:
            bias = bn_ref[0, _near_tile(t, qi)]
            s1 = s1 + bias
            s2 = s2 + bias
        _softmax_pair_step(s1, s2, vl_ref[rows, :], vr_ref[rows, :], m_ref, l_ref, acc_ref)

    _causal_sweep(qi, scores, update)

    li = li_ref[...]
    lam = _diff_lambda(lq1_ref[...], lk1_ref[...], lq2_ref[...], lk2_ref[...], li)
    o = _diff_finish(acc_ref[:, :DIFF_V_DIM] / l_ref[0], acc_ref[:, DIFF_V_DIM:] / l_ref[1], lam,
                     sub_ref[...], li)
    o_ref[0] = o.astype(o_ref.dtype)


def _diff_meta_kernel(q_ref, k_ref, v_ref, bm_ref, lq1_ref, lk1_ref, lq2_ref, lk2_ref, sub_ref, li_ref,
                      prev_ref, o_ref):
    del prev_ref
    li = li_ref[...]
    lam = _diff_lambda(lq1_ref[...], lk1_ref[...], lq2_ref[...], lk2_ref[...], li)
    for hd in range(DIFF_HEADS):
        sl = slice(hd * LANE, (hd + 1) * LANE)
        q1, q2 = _split_diff_q(q_ref[0, :, sl])
        k = k_ref[0, :, sl]
        v = v_ref[0, :, sl]
        outs = []
        for qz in (q1, q2):
            s = _dot_nt(qz, k) + bm_ref[hd]
            p = jnp.exp(s - jnp.max(s, axis=1, keepdims=True))
            outs.append(_dot(p.astype(BF16), v) / jnp.sum(p, axis=1, keepdims=True))
        o_ref[0, :, sl] = _diff_finish(outs[0], outs[1], lam, sub_ref[...], li).astype(o_ref.dtype)


def _diff_attention(qkv, bias, lq1, lk1, lq2, lk2, subln, lam_init, seq, tq):
    bsz, lp, _ = qkv.shape
    nq = seq // tq
    mb = seq // META_ROWS
    hq = DIFF_HEADS
    vec = lambda n: pl.BlockSpec((1, n), lambda *_: (0, 0))
    small = [vec(DIFF_HEAD_DIM)] * 4 + [vec(DIFF_V_DIM), vec(1)]
    smalls = (lq1, lk1, lq2, lk2, subln, lam_init)
    out = pl.pallas_call(
        _diff_kernel,
        out_shape=jax.ShapeDtypeStruct((bsz, lp, hq * DIFF_V_DIM), BF16),
        grid=(bsz, hq, nq),
        in_specs=[
            pl.BlockSpec((1, tq, LANE), lambda b, h, i: (b, i, h)),
            pl.BlockSpec((1, seq, LANE), lambda b, h, i: (b, 0, hq + h)),
            pl.BlockSpec((1, seq, LANE), lambda b, h, i: (b, 0, 2 * hq + h)),
            pl.BlockSpec((1, META_ROWS, LANE), lambda b, h, i: (b, mb, hq + h)),
            pl.BlockSpec((1, META_ROWS, LANE), lambda b, h, i: (b, mb, 2 * hq + h)),
            pl.BlockSpec((1, 3, tq, tq), lambda b, h, i: (h, 0, 0, 0)),
            pl.BlockSpec((1, 1, tq, META_ROWS), lambda b, h, i: (h, jnp.minimum(i, 1), 0, 0)),
        ] + small + [pl.BlockSpec(memory_space=pl.ANY)],
        out_specs=pl.BlockSpec((1, tq, LANE), lambda b, h, i: (b, i, h)),
        input_output_aliases={13: 0},
        scratch_shapes=[pltpu.VMEM((2, tq, LANE), F32), pltpu.VMEM((2, tq, LANE), F32),
                        pltpu.VMEM((tq, 2 * DIFF_V_DIM), F32), pltpu.VMEM((2, 2, tq, tq), F32),
                        pltpu.VMEM((seq, 2 * DIFF_V_DIM), BF16), pltpu.VMEM((seq, 2 * DIFF_V_DIM), BF16)],
        compiler_params=_params("parallel", "parallel", "arbitrary"),
        name="diff_attention",
    )(qkv, qkv, qkv, qkv, qkv, bias["near"], bias["meta_keys"], *smalls,
      jnp.zeros((bsz, lp, hq * DIFF_V_DIM), BF16))

    wide = hq * LANE
    return pl.pallas_call(
        _diff_meta_kernel,
        out_shape=jax.ShapeDtypeStruct(out.shape, out.dtype),
        grid=(bsz,),
        in_specs=[
            pl.BlockSpec((1, META_ROWS, wide), lambda b: (b, mb, 0)),
            pl.BlockSpec((1, META_ROWS, wide), lambda b: (b, mb, 1)),
            pl.BlockSpec((1, META_ROWS, wide), lambda b: (b, mb, 2)),
            pl.BlockSpec(bias["meta_meta"].shape, lambda b: (0, 0, 0)),
        ] + small + [pl.BlockSpec(memory_space=pl.ANY)],
        out_specs=pl.BlockSpec((1, META_ROWS, wide), lambda b: (b, mb, 0)),
        input_output_aliases={10: 0},
        compiler_params=_params("parallel"),
        name="diff_attention_meta",
    )(qkv, qkv, qkv, bias["meta_meta"], *smalls, out)


def _mla_kernel(q_ref, k_ref, v_ref, km_ref, vm_ref, md_ref, mm_ref, base_ref, o_ref, m_ref, l_ref, acc_ref,
                s_ref, vl_ref, vr_ref):
    del base_ref
    qi = pl.program_id(2)
    tq = q_ref.shape[1]
    qa = q_ref[0, :, :MLA_QK_PAD]
    qb = q_ref[0, :, MLA_QK_PAD:]
    _init_state(m_ref, l_ref, acc_ref)

    @pl.when(qi == 0)
    def _():
        vl, vr = _split_values(v_ref[0])
        vl_ref[...] = vl
        vr_ref[...] = vr

    km = km_ref[0]
    vml, vmr = _split_values(vm_ref[0])
    mm = mm_ref[...]
    _softmax_pair_step(_dot_nt(qa, km[:, :MLA_QK_PAD]) + mm,
                       _dot_nt(qb, km[:, MLA_QK_PAD:]) + mm, vml, vmr, m_ref, l_ref, acc_ref)

    def scores(t, slot):
        rows = pl.ds(pl.multiple_of(t * tq, tq), tq)
        s_ref[slot, 0] = _dot_nt(qa, k_ref[0, rows, :MLA_QK_PAD])
        s_ref[slot, 1] = _dot_nt(qb, k_ref[0, rows, MLA_QK_PAD:])

    def update(t, slot, masked):
        rows = pl.ds(pl.multiple_of(t * tq, tq), tq)
        s1 = s_ref[slot, 0]
        s2 = s_ref[slot, 1]
        if masked:
            mask = md_ref[_near_tile(t, qi)]
            s1 = s1 + mask
            s2 = s2 + mask
        _softmax_pair_step(s1, s2, vl_ref[rows, :], vr_ref[rows, :], m_ref, l_ref, acc_ref)

    _causal_sweep(qi, scores, update)
    o_ref[0] = (acc_ref[...] / jnp.concatenate([l_ref[0], l_ref[1]], axis=1)).astype(o_ref.dtype)


def _mla_meta_kernel(q_ref, k_ref, v_ref, mm_ref, prev_ref, o_ref):
    del prev_ref
    for hd in range(MLA_HEADS):
        qk = slice(hd * MLA_QK_PAD, (hd + 1) * MLA_QK_PAD)
        vs = slice(hd * MLA_V_DIM, (hd + 1) * MLA_V_DIM)
        s = _dot_nt(q_ref[0, :, qk], k_ref[0, :, qk]) + mm_ref[...]
        p = jnp.exp(s - jnp.max(s, axis=1, keepdims=True))
        o = _dot(p.astype(BF16), v_ref[0, :, vs]) / jnp.sum(p, axis=1, keepdims=True)
        o_ref[0, :, vs] = o.astype(o_ref.dtype)


def _mla_attention(qm, km, vm, masks, seq, tq):
    bsz, lp, _ = qm.shape
    nq = seq // tq
    mb = seq // META_ROWS
    out = pl.pallas_call(
        _mla_kernel,
        out_shape=jax.ShapeDtypeStruct((bsz, lp, MLA_HEADS * MLA_V_DIM), BF16),
        grid=(bsz, MLA_HEADS // 2, nq),
        in_specs=[
            pl.BlockSpec((1, tq, 2 * MLA_QK_PAD), lambda b, h, i: (b, i, h)),
            pl.BlockSpec((1, seq, 2 * MLA_QK_PAD), lambda b, h, i: (b, 0, h)),
            pl.BlockSpec((1, seq, 2 * MLA_V_DIM), lambda b, h, i: (b, 0, h)),
            pl.BlockSpec((1, META_ROWS, 2 * MLA_QK_PAD), lambda b, h, i: (b, mb, h)),
            pl.BlockSpec((1, META_ROWS, 2 * MLA_V_DIM), lambda b, h, i: (b, mb, h)),
            pl.BlockSpec((3, tq, tq), lambda b, h, i: (0, 0, 0)),
            pl.BlockSpec((1, META_ROWS), lambda b, h, i: (0, 0)),
            pl.BlockSpec(memory_space=pl.ANY),
        ],
        out_specs=pl.BlockSpec((1, tq, 2 * MLA_V_DIM), lambda b, h, i: (b, i, h)),
        input_output_aliases={7: 0},
        scratch_shapes=[pltpu.VMEM((2, tq, LANE), F32), pltpu.VMEM((2, tq, LANE), F32),
                        pltpu.VMEM((tq, 2 * MLA_V_DIM), F32), pltpu.VMEM((2, 2, tq, tq), F32),
                        pltpu.VMEM((seq, 2 * MLA_V_DIM), BF16), pltpu.VMEM((seq, 2 * MLA_V_DIM), BF16)],
        compiler_params=_params("parallel", "parallel", "arbitrary"),
        name="mla_attention",
    )(qm, km, vm, km, vm, masks["near"], masks["meta_cols"],
      jnp.zeros((bsz, lp, MLA_HEADS * MLA_V_DIM), BF16))

    return pl.pallas_call(
        _mla_meta_kernel,
        out_shape=jax.ShapeDtypeStruct(out.shape, out.dtype),
        grid=(bsz,),
        in_specs=[
            pl.BlockSpec((1, META_ROWS, MLA_HEADS * MLA_QK_PAD), lambda b: (b, mb, 0)),
            pl.BlockSpec((1, META_ROWS, MLA_HEADS * MLA_QK_PAD), lambda b: (b, mb, 0)),
            pl.BlockSpec((1, META_ROWS, MLA_HEADS * MLA_V_DIM), lambda b: (b, mb, 0)),
            pl.BlockSpec((1, META_ROWS), lambda b: (0, 0)),
            pl.BlockSpec(memory_space=pl.ANY),
        ],
        out_specs=pl.BlockSpec((1, META_ROWS, MLA_HEADS * MLA_V_DIM), lambda b: (b, mb, 0)),
        input_output_aliases={4: 0},
        compiler_params=_params("parallel"),
        name="mla_attention_meta",
    )(qm, km, vm, masks["meta_cols"], out)


def _mix_kernel(h_ref, g_ref, xa_ref, xb_ref, xc_ref, ga_ref, gb_ref, gc_ref, wa_ref, wb_ref, wc_ref,
                o_ref, hn_ref):
    @pl.when(pl.program_id(1) == 0)
    def _():
        hn_ref[...] = _rms(h_ref[...], g_ref[...]).astype(BF16)

    hn = hn_ref[...]
    mixed = jax.nn.sigmoid(_dot(hn, ga_ref[...])) * _dot(xa_ref[...], wa_ref[...])
    mixed = mixed + jax.nn.sigmoid(_dot(hn, gb_ref[...])) * _dot(xb_ref[...], wb_ref[...])
    mixed = mixed + jax.nn.sigmoid(_dot(hn, gc_ref[...])) * _dot(xc_ref[...], wc_ref[...])
    o_ref[...] = mixed.astype(o_ref.dtype)


def _gated_mix(h, g, xa, xb, xc, w_gates, wa, wb, wc, tm, tn):
    t, d = h.shape
    nj = d // tn
    branch = lambda x: pl.BlockSpec((tm, x.shape[1]), lambda i, j: (i, 0))
    gate = lambda n: pl.BlockSpec((d, tn), lambda i, j: (0, n * nj + j))
    wbr = lambda w: pl.BlockSpec((w.shape[0], tn), lambda i, j: (0, j))
    return pl.pallas_call(
        _mix_kernel,
        out_shape=jax.ShapeDtypeStruct((t, d), BF16),
        grid=(t // tm, nj),
        in_specs=[
            pl.BlockSpec((tm, d), lambda i, j: (i, 0)),
            pl.BlockSpec((1, d), lambda i, j: (0, 0)),
            branch(xa), branch(xb), branch(xc),
            gate(0), gate(1), gate(2),
            wbr(wa), wbr(wb), wbr(wc),
        ],
        out_specs=pl.BlockSpec((tm, tn), lambda i, j: (i, j)),
        scratch_shapes=[pltpu.VMEM((tm, d), BF16)],
        compiler_params=_params("parallel", "arbitrary"),
        name="gated_mix",
    )(h, g, xa, xb, xc, w_gates, w_gates, w_gates, wa, wb, wc)


def _residual_mm_kernel(h_ref, x_ref, w_ref, o_ref):
    o_ref[...] = h_ref[...] + _dot(x_ref[...], w_ref[...])


def _residual_matmul(h, x, w, tm, tn):
    t, d = h.shape
    k = x.shape[1]
    return pl.pallas_call(
        _residual_mm_kernel,
        out_shape=jax.ShapeDtypeStruct((t, d), F32),
        grid=(t // tm, d // tn),
        in_specs=[
            pl.BlockSpec((tm, tn), lambda i, j: (i, j)),
            pl.BlockSpec((tm, k), lambda i, j: (i, 0)),
            pl.BlockSpec((k, tn), lambda i, j: (0, j)),
        ],
        out_specs=pl.BlockSpec((tm, tn), lambda i, j: (i, j)),
        compiler_params=_params("parallel", "arbitrary"),
        name="out_proj",
    )(h, x, w)


def _ffn_kernel(h_ref, g_ref, wg_ref, wu_ref, wd_ref, o_ref, hn_ref):
    f = pl.program_id(1)

    @pl.when(f == 0)
    def _():
        hn_ref[...] = _rms(h_ref[...], g_ref[...]).astype(BF16)

    hn = hn_ref[...]
    gate = _dot(hn, wg_ref[...])
    act = (gate * jax.nn.sigmoid(gate) * _dot(hn, wu_ref[...])).astype(BF16)
    part = _dot(act, wd_ref[...])

    @pl.when(f == 0)
    def _():
        o_ref[...] = h_ref[...] + part

    @pl.when(f > 0)
    def _():
        o_ref[...] += part


def _ffn(h, g, wg, wu, wd, tm, tf):
    t, d = h.shape
    dff = wg.shape[1]
    return pl.pallas_call(
        _ffn_kernel,
        out_shape=jax.ShapeDtypeStruct((t, d), F32),
        grid=(t // tm, dff // tf),
        in_specs=[
            pl.BlockSpec((tm, d), lambda i, f: (i, 0)),
            pl.BlockSpec((1, d), lambda i, f: (0, 0)),
            pl.BlockSpec((d, tf), lambda i, f: (0, f)),
            pl.BlockSpec((d, tf), lambda i, f: (0, f)),
            pl.BlockSpec((tf, d), lambda i, f: (f, 0)),
        ],
        out_specs=pl.BlockSpec((tm, d), lambda i, f: (i, 0)),
        scratch_shapes=[pltpu.VMEM((tm, d), BF16)],
        compiler_params=_params("parallel", "arbitrary"),
        name="swiglu_ffn",
    )(h, g, wg, wu, wd)


def _final_norm_kernel(h_ref, g_ref, o_ref):
    o_ref[0] = _rms(h_ref[0], g_ref[...])


def _final_norm(h3, g, seq, tr):
    bsz, _, d = h3.shape
    return pl.pallas_call(
        _final_norm_kernel,
        out_shape=jax.ShapeDtypeStruct((bsz, seq, d), F32),
        grid=(bsz, seq // tr),
        in_specs=[
            pl.BlockSpec((1, tr, d), lambda b, i: (b, i, 0)),
            pl.BlockSpec((1, d), lambda b, i: (0, 0)),
        ],
        out_specs=pl.BlockSpec((1, tr, d), lambda b, i: (b, i, 0)),
        compiler_params=_params("parallel", "parallel"),
        name="final_norm",
    )(h3, g)


def _t5_bucket(rel):
    nb = REL_BUCKETS // 2
    ret = jnp.where(rel > 0, nb, 0)
    n = jnp.abs(rel)
    max_exact = nb // 2
    nf = jnp.maximum(n, 1).astype(F32)
    large = max_exact + (jnp.log(nf / max_exact) / math.log(REL_MAX_DIST / max_exact)
                         * (nb - max_exact)).astype(jnp.int32)
    large = jnp.minimum(large, nb - 1)
    return ret + jnp.where(n < max_exact, n, large)


def _position_tables(rel_bias, seq, tq):
    far = rel_bias[_t5_bucket(jnp.asarray(-REL_MAX_DIST, jnp.int32))]
    table = (rel_bias - far[None, :]).astype(F32)

    def bias(rel, visible):
        bucket = _t5_bucket(rel)[None]
        t = jnp.zeros((DIFF_HEADS,) + rel.shape, F32)
        for b in range(REL_BUCKETS):
            t = t + jnp.where(bucket == b, table[b][:, None, None], 0.0)
        return jnp.where(visible[None], t, NEG_INF)

    r = jnp.arange(tq, dtype=jnp.int32)[:, None]
    c = jnp.arange(tq, dtype=jnp.int32)[None, :]
    chunk_vis = (c // CHUNK) <= (r // CHUNK)
    all_vis = jnp.ones((tq, tq), bool)
    m = jnp.arange(META_ROWS, dtype=jnp.int32)
    meta_vis = jnp.broadcast_to((m < N_META)[None, :], (tq, META_ROWS))
    diff = {
        "near": jnp.stack([jnp.zeros((DIFF_HEADS, tq, tq), F32), bias(c - tq - r, all_vis),
                           bias(c - r, chunk_vis)], axis=1),
        "meta_keys": jnp.stack([bias(m[None, :] - N_META - r, meta_vis),
                                bias(jnp.full((tq, META_ROWS), -REL_MAX_DIST - tq, jnp.int32), meta_vis)],
                               axis=1),
        "meta_meta": bias(m[None, :] - m[:, None],
                          jnp.broadcast_to((m < N_META)[None, :], (META_ROWS, META_ROWS))),
    }
    mla = {
        "near": jnp.stack([jnp.zeros((tq, tq), F32), jnp.zeros((tq, tq), F32),
                           jnp.where(chunk_vis, 0.0, NEG_INF).astype(F32)]),
        "meta_cols": jnp.where(m < N_META, 0.0, NEG_INF).astype(F32)[None, :],
    }
    return diff, mla


def _rope_tables(seq):
    half = MLA_ROPE_DIM // 2
    row = jnp.arange(seq + META_ROWS, dtype=jnp.int32)
    pos = jnp.where(row < seq, row + N_META, row - seq)
    inv = ROPE_THETA ** (-jnp.arange(half, dtype=F32) / half)
    ang = pos.astype(F32)[:, None] * inv[None, :]
    cos, sin = jnp.cos(ang), jnp.sin(ang)
    pad = jnp.zeros((seq + META_ROWS, LANE - MLA_ROPE_DIM), F32)
    return (jnp.concatenate([cos, cos, pad], axis=1), jnp.concatenate([-sin, sin, pad], axis=1))


def _swap_halves(w):
    half = w.shape[-1] // 2
    return jnp.concatenate([w[..., half:], w[..., :half]], axis=-1)


def _lane_pad(w):
    return jnp.concatenate([w, jnp.zeros(w.shape[:-1] + (LANE - w.shape[-1],), w.dtype)], axis=-1)


def kernel(x, meta_tokens, rel_bias, norm_mix, w_in, conv_w, conv_b, conv_ln_g, conv_ln_b, w_br_conv,
           diff_lq1, diff_lk1, diff_lq2, diff_lk2, diff_subln, w_br_diff, mla_q_norm, w_uq, mla_kv_norm,
           w_ukv, w_br_mla, w_out, norm_ffn, w_ffn_gate, w_ffn_up, w_ffn_down, final_norm):
    bsz, seq, d = x.shape
    depth = w_in.shape[0]
    dff = w_ffn_gate.shape[-1]
    lp = seq + META_ROWS
    t = bsz * lp
    assert seq % META_ROWS == 0 and d % LANE == 0

    tq = _tile(seq, 512)
    tm = _tile(lp, 640)
    tn = _tile(d, 1024)
    tf = _tile(dff, 512)
    tn_mix = _tile(d, 512)
    tn_out = _tile(d, 2048)

    w_val = w_in[:, :, OFF_A:OFF_A + CONV_WIDTH].astype(BF16)
    w_gate = w_in[:, :, OFF_A + CONV_WIDTH:OFF_BQ].astype(BF16)
    w_qkv = w_in[:, :, OFF_BQ:OFF_CQ].astype(BF16)
    w_kr = w_in[:, :, OFF_CR:OFF_G]
    w_c = jnp.concatenate([w_in[:, :, OFF_CQ:OFF_CR], _lane_pad(w_kr), _lane_pad(_swap_halves(w_kr))],
                          axis=-1).astype(BF16)
    w_gates = w_in[:, :, OFF_G:].astype(BF16)
    uq = w_uq.astype(BF16).reshape(depth, MLA_Q_RANK, MLA_HEADS, MLA_NOPE_DIM + MLA_ROPE_DIM)
    uq_rope = uq[..., MLA_NOPE_DIM:]
    w_uq_p = jnp.concatenate([uq[..., :MLA_NOPE_DIM], _lane_pad(uq_rope), _lane_pad(_swap_halves(uq_rope))],
                             axis=-1).reshape(depth, MLA_Q_RANK, MLA_HEADS * MLA_UQ_PAD)
    w_ukv_b = w_ukv.astype(BF16)
    w_bra, w_brb, w_brc = w_br_conv.astype(BF16), w_br_diff.astype(BF16), w_br_mla.astype(BF16)
    w_out_b = w_out.astype(BF16)
    w_fg, w_fu, w_fd = w_ffn_gate.astype(BF16), w_ffn_up.astype(BF16), w_ffn_down.astype(BF16)

    diff_tabs, mla_tabs = _position_tables(rel_bias, seq, tq)
    cos_t, sin_t = _rope_tables(seq)

    row = lambda v: v.reshape(1, -1)
    h = jnp.concatenate([x, jnp.broadcast_to(meta_tokens[None].astype(x.dtype), (bsz, N_META, d)),
                         jnp.zeros((bsz, META_ROWS - N_META, d), x.dtype)], axis=1).reshape(t, d)

    for l in range(depth):
        lam_init = jnp.full((1, 1), 0.8 - 0.6 * math.exp(-0.3 * l), F32)
        g_mix = row(norm_mix[l])
        a = _norm_glu(h, g_mix, w_val[l], w_gate[l], tm, tn)
        qkv = _norm_matmul(h, g_mix, w_qkv[l], BF16, tm, tn)
        qm, km, vm = _mla_proj(h, g_mix, w_c[l], row(mla_q_norm[l]), row(mla_kv_norm[l]), w_uq_p[l],
                               w_ukv_b[l], cos_t, sin_t, tm)
        ya = _conv_branch(a.reshape(bsz, lp, CONV_WIDTH), conv_w[l], row(conv_b[l]), row(conv_ln_g[l]),
                          row(conv_ln_b[l]), seq)
        yb = _diff_attention(qkv.reshape(bsz, lp, -1), diff_tabs, row(diff_lq1[l]), row(diff_lk1[l]),
                             row(diff_lq2[l]), row(diff_lk2[l]), row(diff_subln[l]), lam_init, seq, tq)
        yc = _mla_attention(qm.reshape(bsz, lp, -1), km.reshape(bsz, lp, -1), vm.reshape(bsz, lp, -1),
                            mla_tabs, seq, tq)
        mixed = _gated_mix(h, g_mix, ya.reshape(t, -1), yb.reshape(t, -1), yc.reshape(t, -1),
                           w_gates[l], w_bra[l], w_brb[l], w_brc[l], tm, tn_mix)
        h = _residual_matmul(h, mixed, w_out_b[l], tm, tn_out)
        h = _ffn(h, row(norm_ffn[l]), w_fg[l], w_fu[l], w_fd[l], tm, tf)

    return _final_norm(h.reshape(bsz, lp, d), row(final_norm), seq, _tile(seq, 512))
```

```python
import functools
import math

import jax
import jax.numpy as jnp
from jax import lax
from jax.experimental import pallas as pl
from jax.experimental.pallas import tpu as pltpu

F32 = jnp.float32
BF16 = jnp.bfloat16

CHUNK = 64
N_META = 16
NORM_EPS = 1e-6
NEG_INF = -1e30

CONV_WIDTH = 1024
CONV_KERNEL = 31
DIFF_HEADS = 8
DIFF_HEAD_DIM = 64
DIFF_V_DIM = 2 * DIFF_HEAD_DIM
REL_BUCKETS = 32
REL_MAX_DIST = 128
MLA_HEADS = 8
MLA_Q_RANK = 512
MLA_KV_RANK = 256
MLA_NOPE_DIM = 128
MLA_ROPE_DIM = 64
MLA_V_DIM = 128
ROPE_THETA = 10000.0

DIFF_QK = DIFF_HEADS * 2 * DIFF_HEAD_DIM
OFF_A = 0
OFF_BQ = OFF_A + 2 * CONV_WIDTH
OFF_BK = OFF_BQ + DIFF_QK
OFF_BV = OFF_BK + DIFF_QK
OFF_CQ = OFF_BV + DIFF_HEADS * DIFF_V_DIM
OFF_CKV = OFF_CQ + MLA_Q_RANK
OFF_CR = OFF_CKV + MLA_KV_RANK
OFF_G = OFF_CR + MLA_ROPE_DIM

LANE = 128
META_ROWS = 128
MLA_QK_PAD = 2 * LANE
MLA_UQ_PAD = 3 * LANE
MLA_SCALE = (MLA_NOPE_DIM + MLA_ROPE_DIM) ** -0.5
VMEM_LIMIT = 56 * 1024 * 1024


def _params(*sem):
    return pltpu.CompilerParams(dimension_semantics=sem, vmem_limit_bytes=VMEM_LIMIT)


def _tile(n, cap, unit=LANE):
    best = None
    for t in range(unit, min(n, cap) + 1, unit):
        if n % t == 0:
            best = t
    assert best is not None, (n, cap, unit)
    return best


def _rms(x, g):
    ms = jnp.mean(x * x, axis=-1, keepdims=True)
    return x * lax.rsqrt(ms + NORM_EPS) * g


def _dot(a, b):
    return jnp.dot(a, b, preferred_element_type=F32)


def _dot_nt(a, b):
    return lax.dot_general(a, b, (((1,), (1,)), ((), ())), preferred_element_type=F32)


def _norm_mm_kernel(h_ref, g_ref, w_ref, o_ref, hn_ref):
    @pl.when(pl.program_id(1) == 0)
    def _():
        hn_ref[...] = _rms(h_ref[...], g_ref[...]).astype(BF16)

    o_ref[...] = _dot(hn_ref[...], w_ref[...]).astype(o_ref.dtype)


def _norm_matmul(h, g, w, out_dtype, tm, tn):
    t, d = h.shape
    n = w.shape[1]
    return pl.pallas_call(
        _norm_mm_kernel,
        out_shape=jax.ShapeDtypeStruct((t, n), out_dtype),
        grid=(t // tm, n // tn),
        in_specs=[
            pl.BlockSpec((tm, d), lambda i, j: (i, 0)),
            pl.BlockSpec((1, d), lambda i, j: (0, 0)),
            pl.BlockSpec((d, tn), lambda i, j: (0, j)),
        ],
        out_specs=pl.BlockSpec((tm, tn), lambda i, j: (i, j)),
        scratch_shapes=[pltpu.VMEM((tm, d), BF16)],
        compiler_params=_params("parallel", "arbitrary"),
        name="norm_qkv",
    )(h, g, w)


def _glu_kernel(h_ref, g_ref, wv_ref, wg_ref, o_ref, hn_ref):
    @pl.when(pl.program_id(1) == 0)
    def _():
        hn_ref[...] = _rms(h_ref[...], g_ref[...]).astype(BF16)

    hn = hn_ref[...]
    o_ref[...] = _dot(hn, wv_ref[...]) * jax.nn.sigmoid(_dot(hn, wg_ref[...]))


def _norm_glu(h, g, w_val, w_gate, tm, tn):
    t, d = h.shape
    n = w_val.shape[1]
    return pl.pallas_call(
        _glu_kernel,
        out_shape=jax.ShapeDtypeStruct((t, n), F32),
        grid=(t // tm, n // tn),
        in_specs=[
            pl.BlockSpec((tm, d), lambda i, j: (i, 0)),
            pl.BlockSpec((1, d), lambda i, j: (0, 0)),
            pl.BlockSpec((d, tn), lambda i, j: (0, j)),
            pl.BlockSpec((d, tn), lambda i, j: (0, j)),
        ],
        out_specs=pl.BlockSpec((tm, tn), lambda i, j: (i, j)),
        scratch_shapes=[pltpu.VMEM((tm, d), BF16)],
        compiler_params=_params("parallel", "arbitrary"),
        name="norm_glu",
    )(h, g, w_val, w_gate)


def _mla_proj_kernel(h_ref, g_ref, wc_ref, gq_ref, gkv_ref, wuq_ref, wukv_ref, cos_ref, sin_ref,
                     q_ref, k_ref, v_ref):
    hn = _rms(h_ref[...], g_ref[...]).astype(BF16)
    c = _dot(hn, wc_ref[...])
    cos = cos_ref[...]
    sin = sin_ref[...]
    o_kr = MLA_Q_RANK + MLA_KV_RANK
    kr = (c[:, o_kr:o_kr + LANE] * cos + c[:, o_kr + LANE:o_kr + 2 * LANE] * sin).astype(BF16)
    cqn = _rms(c[:, :MLA_Q_RANK], gq_ref[...]).astype(BF16)
    ckvn = _rms(c[:, MLA_Q_RANK:o_kr], gkv_ref[...]).astype(BF16)
    for hd in range(MLA_HEADS):
        q = _dot(cqn, wuq_ref[:, hd * MLA_UQ_PAD:(hd + 1) * MLA_UQ_PAD]) * MLA_SCALE
        kv = _dot(ckvn, wukv_ref[:, hd * 2 * LANE:(hd + 1) * 2 * LANE])
        qo = hd * MLA_QK_PAD
        q_ref[:, qo:qo + LANE] = q[:, :LANE].astype(BF16)
        q_ref[:, qo + LANE:qo + 2 * LANE] = (q[:, LANE:2 * LANE] * cos + q[:, 2 * LANE:] * sin).astype(BF16)
        k_ref[:, qo:qo + LANE] = kv[:, :LANE].astype(BF16)
        k_ref[:, qo + LANE:qo + 2 * LANE] = kr
        v_ref[:, hd * LANE:(hd + 1) * LANE] = kv[:, LANE:].astype(BF16)


def _mla_proj(h, g, wc, gq, gkv, wuq, wukv, cos_t, sin_t, tm):
    t, d = h.shape
    lp = cos_t.shape[0]
    nper = lp // tm
    const = lambda i: (0, 0)
    return pl.pallas_call(
        _mla_proj_kernel,
        out_shape=(
            jax.ShapeDtypeStruct((t, MLA_HEADS * MLA_QK_PAD), BF16),
            jax.ShapeDtypeStruct((t, MLA_HEADS * MLA_QK_PAD), BF16),
            jax.ShapeDtypeStruct((t, MLA_HEADS * MLA_V_DIM), BF16),
        ),
        grid=(t // tm,),
        in_specs=[
            pl.BlockSpec((tm, d), lambda i: (i, 0)),
            pl.BlockSpec((1, d), const),
            pl.BlockSpec(wc.shape, const),
            pl.BlockSpec(gq.shape, const),
            pl.BlockSpec(gkv.shape, const),
            pl.BlockSpec(wuq.shape, const),
            pl.BlockSpec(wukv.shape, const),
            pl.BlockSpec((tm, LANE), lambda i: (i % nper, 0)),
            pl.BlockSpec((tm, LANE), lambda i: (i % nper, 0)),
        ],
        out_specs=(
            pl.BlockSpec((tm, MLA_HEADS * MLA_QK_PAD), lambda i: (i, 0)),
            pl.BlockSpec((tm, MLA_HEADS * MLA_QK_PAD), lambda i: (i, 0)),
            pl.BlockSpec((tm, MLA_HEADS * MLA_V_DIM), lambda i: (i, 0)),
        ),
        compiler_params=_params("parallel"),
        name="mla_proj",
    )(h, g, wc, gq, gkv, wuq, wukv, cos_t, sin_t)


CONV_HALO = 32
CONV_ROWS = 32
SUBLANES = 8


def _conv_kernel(cur_ref, halo_ref, w_ref, b_ref, g_ref, beta_ref, o_ref, xs_ref, xr_ref, *, n_real_tiles):
    t = pl.program_id(1)
    tc = cur_ref.shape[1]
    c = cur_ref.shape[2]

    @pl.when(t == n_real_tiles)
    def _():
        xs_ref[0:CONV_HALO, :] = jnp.zeros((CONV_HALO, c), F32)

    @pl.when(t == 0)
    def _():
        xs_ref[0:CONV_HALO - N_META, :] = jnp.zeros((CONV_HALO - N_META, c), F32)
        xs_ref[CONV_HALO - N_META:CONV_HALO, :] = halo_ref[0, 0:N_META, :]

    @pl.when((t > 0) & (t < n_real_tiles))
    def _():
        xs_ref[0:CONV_HALO, :] = halo_ref[0]

    xs_ref[CONV_HALO:, :] = cur_ref[0]

    n_shift = CONV_HALO + tc - SUBLANES
    for r in range(1, SUBLANES):
        xr_ref[r - 1, 0:n_shift, :] = xs_ref[r:r + n_shift, :]

    first_tap = CONV_HALO - (CONV_KERNEL - 1)
    for rc in range(tc // CONV_ROWS):
        base = rc * CONV_ROWS
        nt = CONV_ROWS // SUBLANES
        acc = jnp.broadcast_to(b_ref[...][None], (nt, SUBLANES, c))
        for k in range(CONV_KERNEL):
            shift = (first_tap + k) % SUBLANES
            lo = base + (first_tap + k) - shift
            if shift == 0:
                x = xs_ref[lo:lo + CONV_ROWS, :]
            else:
                x = xr_ref[shift - 1, lo:lo + CONV_ROWS, :]
            acc = acc + w_ref[k][None] * x.reshape(nt, SUBLANES, c)
        acc = acc.reshape(CONV_ROWS, c)
        mu = jnp.mean(acc, axis=-1, keepdims=True)
        ctr = acc - mu
        var = jnp.mean(ctr * ctr, axis=-1, keepdims=True)
        y = ctr * lax.rsqrt(var + NORM_EPS) * g_ref[...] + beta_ref[...]
        o_ref[0, base:base + CONV_ROWS, :] = (y * jax.nn.sigmoid(y)).astype(o_ref.dtype)


def _conv_branch(a, w, b, g, beta, seq):
    bsz, lp, c = a.shape
    tc = META_ROWS
    n_real = seq // tc
    hpt = tc // CONV_HALO

    def halo_map(bi, t):
        inside = (t > 0) & (t < n_real)
        return (bi, jnp.where(inside, t * hpt - 1, seq // CONV_HALO), 0)

    const = lambda bi, t: (0, 0)
    return pl.pallas_call(
        functools.partial(_conv_kernel, n_real_tiles=n_real),
        out_shape=jax.ShapeDtypeStruct((bsz, lp, c), BF16),
        grid=(bsz, n_real + 1),
        in_specs=[
            pl.BlockSpec((1, tc, c), lambda bi, t: (bi, t, 0)),
            pl.BlockSpec((1, CONV_HALO, c), halo_map),
            pl.BlockSpec((CONV_KERNEL, SUBLANES, c), lambda bi, t: (0, 0, 0)),
            pl.BlockSpec((SUBLANES, c), const),
            pl.BlockSpec((1, c), const),
            pl.BlockSpec((1, c), const),
        ],
        out_specs=pl.BlockSpec((1, tc, c), lambda bi, t: (bi, t, 0)),
        scratch_shapes=[pltpu.VMEM((CONV_HALO + tc, c), F32),
                        pltpu.VMEM((SUBLANES - 1, CONV_HALO + tc, c), F32)],
        compiler_params=_params("parallel", "arbitrary"),
        name="conv_ln_swish",
    )(a, a, jnp.broadcast_to(w[:, None, :], (CONV_KERNEL, SUBLANES, c)), jnp.broadcast_to(b, (SUBLANES, c)),
      g, beta)


def _softmax_pair_step(s1, s2, vl, vr, m_ref, l_ref, acc_ref):
    alphas, ps = [], []
    for i, s in enumerate((s1, s2)):
        m_prev = m_ref[i]
        m_new = jnp.maximum(m_prev, jnp.max(s, axis=1, keepdims=True))
        alpha = jnp.exp(m_prev - m_new)
        p = jnp.exp((s - jnp.concatenate([m_new] * (s.shape[1] // LANE), axis=1)).astype(BF16))
        l_ref[i] = alpha * l_ref[i] + jnp.sum(p.astype(F32), axis=1, keepdims=True)
        m_ref[i] = m_new
        alphas.append(alpha)
        ps.append(p)
    acc_ref[...] = (jnp.concatenate(alphas, axis=1) * acc_ref[...]
                    + _dot(ps[0], vl) + _dot(ps[1], vr))


def _split_values(v):
    lane = lax.broadcasted_iota(jnp.int32, v.shape, 1)
    zero = jnp.zeros_like(v)
    return jnp.where(lane < LANE, v, zero), jnp.where(lane < LANE, zero, v)


def _init_state(m_ref, l_ref, acc_ref):
    m_ref[...] = jnp.full(m_ref.shape, NEG_INF, F32)
    l_ref[...] = jnp.zeros(l_ref.shape, F32)
    acc_ref[...] = jnp.zeros(acc_ref.shape, F32)


def _split_diff_q(q):
    lane = lax.broadcasted_iota(jnp.int32, q.shape, 1)
    qs = q * jnp.asarray(DIFF_HEAD_DIM ** -0.5, q.dtype)
    zero = jnp.zeros_like(qs)
    return jnp.where(lane < DIFF_HEAD_DIM, qs, zero), jnp.where(lane < DIFF_HEAD_DIM, zero, qs)


def _diff_lambda(lq1, lk1, lq2, lk2, lam_init):
    return (jnp.exp(jnp.sum(lq1 * lk1, axis=-1, keepdims=True))
            - jnp.exp(jnp.sum(lq2 * lk2, axis=-1, keepdims=True)) + lam_init)


def _diff_finish(o1, o2, lam, subln, lam_init):
    o = o1 - lam * o2
    return _rms(o, subln) * (1.0 - lam_init)


def _causal_sweep(qi, scores, update):
    npairs = jnp.where(qi > 0, lax.shift_right_logical(qi - 1, 1), 0)
    scores(0, 0)

    def pair(u, carry):
        t = 2 * u
        scores(t + 1, 1)
        update(t, 0, False)
        scores(t + 2, 0)
        update(t + 1, 1, False)
        return carry

    lax.fori_loop(0, npairs, pair, 0)

    t0 = 2 * npairs
    scores(jnp.minimum(t0 + 1, qi), 1)
    update(t0, 0, True)

    @pl.when(t0 + 1 <= qi)
    def _():
        scores(jnp.minimum(t0 + 2, qi), 0)
        update(t0 + 1, 1, True)

    @pl.when(t0 + 2 <= qi)
    def _():
        update(t0 + 2, 0, True)


def _near_tile(t, qi):
    return jnp.clip(t - qi + 2, 0, 2)


def _diff_kernel(q_ref, k_ref, v_ref, km_ref, vm_ref, bn_ref, bm_ref,
                 lq1_ref, lk1_ref, lq2_ref, lk2_ref, sub_ref, li_ref, base_ref,
                 o_ref, m_ref, l_ref, acc_ref, s_ref, vl_ref, vr_ref):
    del base_ref
    qi = pl.program_id(2)
    tq = q_ref.shape[1]
    q1, q2 = _split_diff_q(q_ref[0])
    _init_state(m_ref, l_ref, acc_ref)

    def widen(v):
        return _split_values(jnp.concatenate([v, v], axis=1))

    @pl.when(qi == 0)
    def _():
        vl, vr = widen(v_ref[0])
        vl_ref[...] = vl
        vr_ref[...] = vr

    km = km_ref[0]
    vml, vmr = widen(vm_ref[0])
    bm = bm_ref[0, 0]
    _softmax_pair_step(_dot_nt(q1, km) + bm, _dot_nt(q2, km) + bm, vml, vmr, m_ref, l_ref, acc_ref)

    def scores(t, slot):
        k = k_ref[0, pl.ds(pl.multiple_of(t * tq, tq), tq), :]
        s_ref[slot, 0] = _dot_nt(q1, k)
        s_ref[slot, 1] = _dot_nt(q2, k)

    def update(t, slot, biased):
        rows = pl.ds(pl.multiple_of(t * tq, tq), tq)
        s1 = s_ref[slot, 0]
        s2 = s_ref[slot, 1]
        if biased:
            bias = bn_ref[0, _near_tile(t, qi)]
            s1 = s1 + bias
            s2 = s2 + bias
        _softmax_pair_step(s1, s2, vl_ref[rows, :], vr_ref[rows, :], m_ref, l_ref, acc_ref)

    _causal_sweep(qi, scores, update)

    li = li_ref[...]
    lam = _diff_lambda(lq1_ref[...], lk1_ref[...], lq2_ref[...], lk2_ref[...], li)
    o = _diff_finish(acc_ref[:, :DIFF_V_DIM] / l_ref[0], acc_ref[:, DIFF_V_DIM:] / l_ref[1], lam,
                     sub_ref[...], li)
    o_ref[0] = o.astype(o_ref.dtype)


def _diff_meta_kernel(q_ref, k_ref, v_ref, bm_ref, lq1_ref, lk1_ref, lq2_ref, lk2_ref, sub_ref, li_ref,
                      prev_ref, o_ref):
    del prev_ref
    li = li_ref[...]
    lam = _diff_lambda(lq1_ref[...], lk1_ref[...], lq2_ref[...], lk2_ref[...], li)
    for hd in range(DIFF_HEADS):
        sl = slice(hd * LANE, (hd + 1) * LANE)
        q1, q2 = _split_diff_q(q_ref[0, :, sl])
        k = k_ref[0, :, sl]
        v = v_ref[0, :, sl]
        outs = []
        for qz in (q1, q2):
            s = _dot_nt(qz, k) + bm_ref[hd]
            p = jnp.exp(s - jnp.max(s, axis=1, keepdims=True))
            outs.append(_dot(p.astype(BF16), v) / jnp.sum(p, axis=1, keepdims=True))
        o_ref[0, :, sl] = _diff_finish(outs[0], outs[1], lam, sub_ref[...], li).astype(o_ref.dtype)


def _diff_attention(qkv, bias, lq1, lk1, lq2, lk2, subln, lam_init, seq, tq):
    bsz, lp, _ = qkv.shape
    nq = seq // tq
    mb = seq // META_ROWS
    hq = DIFF_HEADS
    vec = lambda n: pl.BlockSpec((1, n), lambda *_: (0, 0))
    small = [vec(DIFF_HEAD_DIM)] * 4 + [vec(DIFF_V_DIM), vec(1)]
    smalls = (lq1, lk1, lq2, lk2, subln, lam_init)
    out = pl.pallas_call(
        _diff_kernel,
        out_shape=jax.ShapeDtypeStruct((bsz, lp, hq * DIFF_V_DIM), BF16),
        grid=(bsz, hq, nq),
        in_specs=[
            pl.BlockSpec((1, tq, LANE), lambda b, h, i: (b, i, h)),
            pl.BlockSpec((1, seq, LANE), lambda b, h, i: (b, 0, hq + h)),
            pl.BlockSpec((1, seq, LANE), lambda b, h, i: (b, 0, 2 * hq + h)),
            pl.BlockSpec((1, META_ROWS, LANE), lambda b, h, i: (b, mb, hq + h)),
            pl.BlockSpec((1, META_ROWS, LANE), lambda b, h, i: (b, mb, 2 * hq + h)),
            pl.BlockSpec((1, 3, tq, tq), lambda b, h, i: (h, 0, 0, 0)),
            pl.BlockSpec((1, 1, tq, META_ROWS), lambda b, h, i: (h, jnp.minimum(i, 1), 0, 0)),
        ] + small + [pl.BlockSpec(memory_space=pl.ANY)],
        out_specs=pl.BlockSpec((1, tq, LANE), lambda b, h, i: (b, i, h)),
        input_output_aliases={13: 0},
        scratch_shapes=[pltpu.VMEM((2, tq, LANE), F32), pltpu.VMEM((2, tq, LANE), F32),
                        pltpu.VMEM((tq, 2 * DIFF_V_DIM), F32), pltpu.VMEM((2, 2, tq, tq), F32),
                        pltpu.VMEM((seq, 2 * DIFF_V_DIM), BF16), pltpu.VMEM((seq, 2 * DIFF_V_DIM), BF16)],
        compiler_params=_params("parallel", "parallel", "arbitrary"),
        name="diff_attention",
    )(qkv, qkv, qkv, qkv, qkv, bias["near"], bias["meta_keys"], *smalls,
      jnp.zeros((bsz, lp, hq * DIFF_V_DIM), BF16))

    wide = hq * LANE
    return pl.pallas_call(
        _diff_meta_kernel,
        out_shape=jax.ShapeDtypeStruct(out.shape, out.dtype),
        grid=(bsz,),
        in_specs=[
            pl.BlockSpec((1, META_ROWS, wide), lambda b: (b, mb, 0)),
            pl.BlockSpec((1, META_ROWS, wide), lambda b: (b, mb, 1)),
            pl.BlockSpec((1, META_ROWS, wide), lambda b: (b, mb, 2)),
            pl.BlockSpec(bias["meta_meta"].shape, lambda b: (0, 0, 0)),
        ] + small + [pl.BlockSpec(memory_space=pl.ANY)],
        out_specs=pl.BlockSpec((1, META_ROWS, wide), lambda b: (b, mb, 0)),
        input_output_aliases={10: 0},
        compiler_params=_params("parallel"),
        name="diff_attention_meta",
    )(qkv, qkv, qkv, bias["meta_meta"], *smalls, out)


def _mla_kernel(q_ref, k_ref, v_ref, km_ref, vm_ref, md_ref, mm_ref, base_ref, o_ref, m_ref, l_ref, acc_ref,
                s_ref, vl_ref, vr_ref):
    del base_ref
    qi = pl.program_id(2)
    tq = q_ref.shape[1]
    qa = q_ref[0, :, :MLA_QK_PAD]
    qb = q_ref[0, :, MLA_QK_PAD:]
    _init_state(m_ref, l_ref, acc_ref)

    @pl.when(qi == 0)
    def _():
        vl, vr = _split_values(v_ref[0])
        vl_ref[...] = vl
        vr_ref[...] = vr

    km = km_ref[0]
    vml, vmr = _split_values(vm_ref[0])
    mm = mm_ref[...]
    _softmax_pair_step(_dot_nt(qa, km[:, :MLA_QK_PAD]) + mm,
                       _dot_nt(qb, km[:, MLA_QK_PAD:]) + mm, vml, vmr, m_ref, l_ref, acc_ref)

    def scores(t, slot):
        rows = pl.ds(pl.multiple_of(t * tq, tq), tq)
        s_ref[slot, 0] = _dot_nt(qa, k_ref[0, rows, :MLA_QK_PAD])
        s_ref[slot, 1] = _dot_nt(qb, k_ref[0, rows, MLA_QK_PAD:])

    def update(t, slot, masked):
        rows = pl.ds(pl.multiple_of(t * tq, tq), tq)
        s1 = s_ref[slot, 0]
        s2 = s_ref[slot, 1]
        if masked:
            mask = md_ref[_near_tile(t, qi)]
            s1 = s1 + mask
            s2 = s2 + mask
        _softmax_pair_step(s1, s2, vl_ref[rows, :], vr_ref[rows, :], m_ref, l_ref, acc_ref)

    _causal_sweep(qi, scores, update)
    o_ref[0] = (acc_ref[...] / jnp.concatenate([l_ref[0], l_ref[1]], axis=1)).astype(o_ref.dtype)


def _mla_meta_kernel(q_ref, k_ref, v_ref, mm_ref, prev_ref, o_ref):
    del prev_ref
    for hd in range(MLA_HEADS):
        qk = slice(hd * MLA_QK_PAD, (hd + 1) * MLA_QK_PAD)
        vs = slice(hd * MLA_V_DIM, (hd + 1) * MLA_V_DIM)
        s = _dot_nt(q_ref[0, :, qk], k_ref[0, :, qk]) + mm_ref[...]
        p = jnp.exp(s - jnp.max(s, axis=1, keepdims=True))
        o = _dot(p.astype(BF16), v_ref[0, :, vs]) / jnp.sum(p, axis=1, keepdims=True)
        o_ref[0, :, vs] = o.astype(o_ref.dtype)


def _mla_attention(qm, km, vm, masks, seq, tq):
    bsz, lp, _ = qm.shape
    nq = seq // tq
    mb = seq // META_ROWS
    out = pl.pallas_call(
        _mla_kernel,
        out_shape=jax.ShapeDtypeStruct((bsz, lp, MLA_HEADS * MLA_V_DIM), BF16),
        grid=(bsz, MLA_HEADS // 2, nq),
        in_specs=[
            pl.BlockSpec((1, tq, 2 * MLA_QK_PAD), lambda b, h, i: (b, i, h)),
            pl.BlockSpec((1, seq, 2 * MLA_QK_PAD), lambda b, h, i: (b, 0, h)),
            pl.BlockSpec((1, seq, 2 * MLA_V_DIM), lambda b, h, i: (b, 0, h)),
            pl.BlockSpec((1, META_ROWS, 2 * MLA_QK_PAD), lambda b, h, i: (b, mb, h)),
            pl.BlockSpec((1, META_ROWS, 2 * MLA_V_DIM), lambda b, h, i: (b, mb, h)),
            pl.BlockSpec((3, tq, tq), lambda b, h, i: (0, 0, 0)),
            pl.BlockSpec((1, META_ROWS), lambda b, h, i: (0, 0)),
            pl.BlockSpec(memory_space=pl.ANY),
        ],
        out_specs=pl.BlockSpec((1, tq, 2 * MLA_V_DIM), lambda b, h, i: (b, i, h)),
        input_output_aliases={7: 0},
        scratch_shapes=[pltpu.VMEM((2, tq, LANE), F32), pltpu.VMEM((2, tq, LANE), F32),
                        pltpu.VMEM((tq, 2 * MLA_V_DIM), F32), pltpu.VMEM((2, 2, tq, tq), F32),
                        pltpu.VMEM((seq, 2 * MLA_V_DIM), BF16), pltpu.VMEM((seq, 2 * MLA_V_DIM), BF16)],
        compiler_params=_params("parallel", "parallel", "arbitrary"),
        name="mla_attention",
    )(qm, km, vm, km, vm, masks["near"], masks["meta_cols"],
      jnp.zeros((bsz, lp, MLA_HEADS * MLA_V_DIM), BF16))

    return pl.pallas_call(
        _mla_meta_kernel,
        out_shape=jax.ShapeDtypeStruct(out.shape, out.dtype),
        grid=(bsz,),
        in_specs=[
            pl.BlockSpec((1, META_ROWS, MLA_HEADS * MLA_QK_PAD), lambda b: (b, mb, 0)),
            pl.BlockSpec((1, META_ROWS, MLA_HEADS * MLA_QK_PAD), lambda b: (b, mb, 0)),
            pl.BlockSpec((1, META_ROWS, MLA_HEADS * MLA_V_DIM), lambda b: (b, mb, 0)),
            pl.BlockSpec((1, META_ROWS), lambda b: (0, 0)),
            pl.BlockSpec(memory_space=pl.ANY),
        ],
        out_specs=pl.BlockSpec((1, META_ROWS, MLA_HEADS * MLA_V_DIM), lambda b: (b, mb, 0)),
        input_output_aliases={4: 0},
        compiler_params=_params("parallel"),
        name="mla_attention_meta",
    )(qm, km, vm, masks["meta_cols"], out)


def _mix_kernel(h_ref, g_ref, xa_ref, xb_ref, xc_ref, ga_ref, gb_ref, gc_ref, wa_ref, wb_ref, wc_ref,
                o_ref, hn_ref):
    @pl.when(pl.program_id(1) == 0)
    def _():
        hn_ref[...] = _rms(h_ref[...], g_ref[...]).astype(BF16)

    hn = hn_ref[...]
    mixed = jax.nn.sigmoid(_dot(hn, ga_ref[...])) * _dot(xa_ref[...], wa_ref[...])
    mixed = mixed + jax.nn.sigmoid(_dot(hn, gb_ref[...])) * _dot(xb_ref[...], wb_ref[...])
    mixed = mixed + jax.nn.sigmoid(_dot(hn, gc_ref[...])) * _dot(xc_ref[...], wc_ref[...])
    o_ref[...] = mixed.astype(o_ref.dtype)


def _gated_mix(h, g, xa, xb, xc, w_gates, wa, wb, wc, tm, tn):
    t, d = h.shape
    nj = d // tn
    branch = lambda x: pl.BlockSpec((tm, x.shape[1]), lambda i, j: (i, 0))
    gate = lambda n: pl.BlockSpec((d, tn), lambda i, j: (0, n * nj + j))
    wbr = lambda w: pl.BlockSpec((w.shape[0], tn), lambda i, j: (0, j))
    return pl.pallas_call(
        _mix_kernel,
        out_shape=jax.ShapeDtypeStruct((t, d), BF16),
        grid=(t // tm, nj),
        in_specs=[
            pl.BlockSpec((tm, d), lambda i, j: (i, 0)),
            pl.BlockSpec((1, d), lambda i, j: (0, 0)),
            branch(xa), branch(xb), branch(xc),
            gate(0), gate(1), gate(2),
            wbr(wa), wbr(wb), wbr(wc),
        ],
        out_specs=pl.BlockSpec((tm, tn), lambda i, j: (i, j)),
        scratch_shapes=[pltpu.VMEM((tm, d), BF16)],
        compiler_params=_params("parallel", "arbitrary"),
        name="gated_mix",
    )(h, g, xa, xb, xc, w_gates, w_gates, w_gates, wa, wb, wc)


def _residual_mm_kernel(h_ref, x_ref, w_ref, o_ref):
    o_ref[...] = h_ref[...] + _dot(x_ref[...], w_ref[...])


def _residual_matmul(h, x, w, tm, tn):
    t, d = h.shape
    k = x.shape[1]
    return pl.pallas_call(
        _residual_mm_kernel,
        out_shape=jax.ShapeDtypeStruct((t, d), F32),
        grid=(t // tm, d // tn),
        in_specs=[
            pl.BlockSpec((tm, tn), lambda i, j: (i, j)),
            pl.BlockSpec((tm, k), lambda i, j: (i, 0)),
            pl.BlockSpec((k, tn), lambda i, j: (0, j)),
        ],
        out_specs=pl.BlockSpec((tm, tn), lambda i, j: (i, j)),
        compiler_params=_params("parallel", "arbitrary"),
        name="out_proj",
    )(h, x, w)


def _ffn_kernel(h_ref, g_ref, wg_ref, wu_ref, wd_ref, o_ref, hn_ref):
    f = pl.program_id(1)

    @pl.when(f == 0)
    def _():
        h = h_ref[...]
        hn_ref[...] = _rms(h, g_ref[...]).astype(BF16)
        o_ref[...] = h

    hn = hn_ref[...]
    gate = _dot(hn, wg_ref[...])
    act = (gate * jax.nn.sigmoid(gate) * _dot(hn, wu_ref[...])).astype(BF16)
    o_ref[...] += _dot(act, wd_ref[...])


def _ffn(h, g, wg, wu, wd, tm, tf):
    t, d = h.shape
    dff = wg.shape[1]
    return pl.pallas_call(
        _ffn_kernel,
        out_shape=jax.ShapeDtypeStruct((t, d), F32),
        grid=(t // tm, dff // tf),
        in_specs=[
            pl.BlockSpec((tm, d), lambda i, f: (i, 0)),
            pl.BlockSpec((1, d), lambda i, f: (0, 0)),
            pl.BlockSpec((d, tf), lambda i, f: (0, f)),
            pl.BlockSpec((d, tf), lambda i, f: (0, f)),
            pl.BlockSpec((tf, d), lambda i, f: (f, 0)),
        ],
        out_specs=pl.BlockSpec((tm, d), lambda i, f: (i, 0)),
        scratch_shapes=[pltpu.VMEM((tm, d), BF16)],
        compiler_params=_params("parallel", "arbitrary"),
        name="swiglu_ffn",
    )(h, g, wg, wu, wd)


def _final_norm_kernel(h_ref, g_ref, o_ref):
    o_ref[0] = _rms(h_ref[0], g_ref[...])


def _final_norm(h3, g, seq, tr):
    bsz, _, d = h3.shape
    return pl.pallas_call(
        _final_norm_kernel,
        out_shape=jax.ShapeDtypeStruct((bsz, seq, d), F32),
        grid=(bsz, seq // tr),
        in_specs=[
            pl.BlockSpec((1, tr, d), lambda b, i: (b, i, 0)),
            pl.BlockSpec((1, d), lambda b, i: (0, 0)),
        ],
        out_specs=pl.BlockSpec((1, tr, d), lambda b, i: (b, i, 0)),
        compiler_params=_params("parallel", "parallel"),
        name="final_norm",
    )(h3, g)


def _t5_bucket(rel):
    nb = REL_BUCKETS // 2
    ret = jnp.where(rel > 0, nb, 0)
    n = jnp.abs(rel)
    max_exact = nb // 2
    nf = jnp.maximum(n, 1).astype(F32)
    large = max_exact + (jnp.log(nf / max_exact) / math.log(REL_MAX_DIST / max_exact)
                         * (nb - max_exact)).astype(jnp.int32)
    large = jnp.minimum(large, nb - 1)
    return ret + jnp.where(n < max_exact, n, large)


def _position_tables(rel_bias, seq, tq):
    far = rel_bias[_t5_bucket(jnp.asarray(-REL_MAX_DIST, jnp.int32))]
    table = (rel_bias - far[None, :]).astype(F32)

    def bias(rel, visible):
        bucket = _t5_bucket(rel)[None]
        t = jnp.zeros((DIFF_HEADS,) + rel.shape, F32)
        for b in range(REL_BUCKETS):
            t = t + jnp.where(bucket == b, table[b][:, None, None], 0.0)
        return jnp.where(visible[None], t, NEG_INF)

    r = jnp.arange(tq, dtype=jnp.int32)[:, None]
    c = jnp.arange(tq, dtype=jnp.int32)[None, :]
    chunk_vis = (c // CHUNK) <= (r // CHUNK)
    all_vis = jnp.ones((tq, tq), bool)
    m = jnp.arange(META_ROWS, dtype=jnp.int32)
    meta_vis = jnp.broadcast_to((m < N_META)[None, :], (tq, META_ROWS))
    diff = {
        "near": jnp.stack([jnp.zeros((DIFF_HEADS, tq, tq), F32), bias(c - tq - r, all_vis),
                           bias(c - r, chunk_vis)], axis=1),
        "meta_keys": jnp.stack([bias(m[None, :] - N_META - r, meta_vis),
                                bias(jnp.full((tq, META_ROWS), -REL_MAX_DIST - tq, jnp.int32), meta_vis)],
                               axis=1),
        "meta_meta": bias(m[None, :] - m[:, None],
                          jnp.broadcast_to((m < N_META)[None, :], (META_ROWS, META_ROWS))),
    }
    mla = {
        "near": jnp.stack([jnp.zeros((tq, tq), F32), jnp.zeros((tq, tq), F32),
                           jnp.where(chunk_vis, 0.0, NEG_INF).astype(F32)]),
        "meta_cols": jnp.where(m < N_META, 0.0, NEG_INF).astype(F32)[None, :],
    }
    return diff, mla


def _rope_tables(seq):
    half = MLA_ROPE_DIM // 2
    row = jnp.arange(seq + META_ROWS, dtype=jnp.int32)
    pos = jnp.where(row < seq, row + N_META, row - seq)
    inv = ROPE_THETA ** (-jnp.arange(half, dtype=F32) / half)
    ang = pos.astype(F32)[:, None] * inv[None, :]
    cos, sin = jnp.cos(ang), jnp.sin(ang)
    pad = jnp.zeros((seq + META_ROWS, LANE - MLA_ROPE_DIM), F32)
    return (jnp.concatenate([cos, cos, pad], axis=1), jnp.concatenate([-sin, sin, pad], axis=1))


def _swap_halves(w):
    half = w.shape[-1] // 2
    return jnp.concatenate([w[..., half:], w[..., :half]], axis=-1)


def _lane_pad(w):
    return jnp.concatenate([w, jnp.zeros(w.shape[:-1] + (LANE - w.shape[-1],), w.dtype)], axis=-1)


def kernel(x, meta_tokens, rel_bias, norm_mix, w_in, conv_w, conv_b, conv_ln_g, conv_ln_b, w_br_conv,
           diff_lq1, diff_lk1, diff_lq2, diff_lk2, diff_subln, w_br_diff, mla_q_norm, w_uq, mla_kv_norm,
           w_ukv, w_br_mla, w_out, norm_ffn, w_ffn_gate, w_ffn_up, w_ffn_down, final_norm):
    bsz, seq, d = x.shape
    depth = w_in.shape[0]
    dff = w_ffn_gate.shape[-1]
    lp = seq + META_ROWS
    t = bsz * lp
    assert seq % META_ROWS == 0 and d % LANE == 0

    tq = _tile(seq, 512)
    tm = _tile(lp, 640)
    tn = _tile(d, 1024)
    tf = _tile(dff, 512)
    tn_mix = _tile(d, 512)
    tn_out = _tile(d, 2048)
    tn_qkv = _tile(OFF_CQ - OFF_BQ, 1536)

    w_val = w_in[:, :, OFF_A:OFF_A + CONV_WIDTH].astype(BF16)
    w_gate = w_in[:, :, OFF_A + CONV_WIDTH:OFF_BQ].astype(BF16)
    w_qkv = w_in[:, :, OFF_BQ:OFF_CQ].astype(BF16)
    w_kr = w_in[:, :, OFF_CR:OFF_G]
    w_c = jnp.concatenate([w_in[:, :, OFF_CQ:OFF_CR], _lane_pad(w_kr), _lane_pad(_swap_halves(w_kr))],
                          axis=-1).astype(BF16)
    w_gates = w_in[:, :, OFF_G:].astype(BF16)
    uq = w_uq.astype(BF16).reshape(depth, MLA_Q_RANK, MLA_HEADS, MLA_NOPE_DIM + MLA_ROPE_DIM)
    uq_rope = uq[..., MLA_NOPE_DIM:]
    w_uq_p = jnp.concatenate([uq[..., :MLA_NOPE_DIM], _lane_pad(uq_rope), _lane_pad(_swap_halves(uq_rope))],
                             axis=-1).reshape(depth, MLA_Q_RANK, MLA_HEADS * MLA_UQ_PAD)
    w_ukv_b = w_ukv.astype(BF16)
    w_bra, w_brb, w_brc = w_br_conv.astype(BF16), w_br_diff.astype(BF16), w_br_mla.astype(BF16)
    w_out_b = w_out.astype(BF16)
    w_fg, w_fu, w_fd = w_ffn_gate.astype(BF16), w_ffn_up.astype(BF16), w_ffn_down.astype(BF16)

    diff_tabs, mla_tabs = _position_tables(rel_bias, seq, tq)
    cos_t, sin_t = _rope_tables(seq)

    row = lambda v: v.reshape(1, -1)
    h = jnp.concatenate([x, jnp.broadcast_to(meta_tokens[None].astype(x.dtype), (bsz, N_META, d)),
                         jnp.zeros((bsz, META_ROWS - N_META, d), x.dtype)], axis=1).reshape(t, d)

    for l in range(depth):
        lam_init = jnp.full((1, 1), 0.8 - 0.6 * math.exp(-0.3 * l), F32)
        g_mix = row(norm_mix[l])
        a = _norm_glu(h, g_mix, w_val[l], w_gate[l], tm, tn)
        qkv = _norm_matmul(h, g_mix, w_qkv[l], BF16, tm, tn_qkv)
        qm, km, vm = _mla_proj(h, g_mix, w_c[l], row(mla_q_norm[l]), row(mla_kv_norm[l]), w_uq_p[l],
                               w_ukv_b[l], cos_t, sin_t, tm)
        ya = _conv_branch(a.reshape(bsz, lp, CONV_WIDTH), conv_w[l], row(conv_b[l]), row(conv_ln_g[l]),
                          row(conv_ln_b[l]), seq)
        yb = _diff_attention(qkv.reshape(bsz, lp, -1), diff_tabs, row(diff_lq1[l]), row(diff_lk1[l]),
                             row(diff_lq2[l]), row(diff_lk2[l]), row(diff_subln[l]), lam_init, seq, tq)
        yc = _mla_attention(qm.reshape(bsz, lp, -1), km.reshape(bsz, lp, -1), vm.reshape(bsz, lp, -1),
                            mla_tabs, seq, tq)
        mixed = _gated_mix(h, g_mix, ya.reshape(t, -1), yb.reshape(t, -1), yc.reshape(t, -1),
                           w_gates[l], w_bra[l], w_brb[l], w_brc[l], tm, tn_mix)
        h = _residual_matmul(h, mixed, w_out_b[l], tm, tn_out)
        h = _ffn(h, row(norm_ffn[l]), w_fg[l], w_fu[l], w_fd[l], tm, tf)

    return _final_norm(h.reshape(bsz, lp, d), row(final_norm), seq, _tile(seq, 512))
```

```python
import functools
import math

import jax
import jax.numpy as jnp
from jax import lax
from jax.experimental import pallas as pl
from jax.experimental.pallas import tpu as pltpu

F32 = jnp.float32
BF16 = jnp.bfloat16

CHUNK = 64
N_META = 16
NORM_EPS = 1e-6
NEG_INF = -1e30

CONV_WIDTH = 1024
CONV_KERNEL = 31
DIFF_HEADS = 8
DIFF_HEAD_DIM = 64
DIFF_V_DIM = 2 * DIFF_HEAD_DIM
REL_BUCKETS = 32
REL_MAX_DIST = 128
MLA_HEADS = 8
MLA_Q_RANK = 512
MLA_KV_RANK = 256
MLA_NOPE_DIM = 128
MLA_ROPE_DIM = 64
MLA_V_DIM = 128
ROPE_THETA = 10000.0

DIFF_QK = DIFF_HEADS * 2 * DIFF_HEAD_DIM
OFF_A = 0
OFF_BQ = OFF_A + 2 * CONV_WIDTH
OFF_BK = OFF_BQ + DIFF_QK
OFF_BV = OFF_BK + DIFF_QK
OFF_CQ = OFF_BV + DIFF_HEADS * DIFF_V_DIM
OFF_CKV = OFF_CQ + MLA_Q_RANK
OFF_CR = OFF_CKV + MLA_KV_RANK
OFF_G = OFF_CR + MLA_ROPE_DIM

LANE = 128
META_ROWS = 128
MLA_QK_PAD = 2 * LANE
MLA_UQ_PAD = 3 * LANE
MLA_SCALE = (MLA_NOPE_DIM + MLA_ROPE_DIM) ** -0.5
VMEM_LIMIT = 56 * 1024 * 1024


def _params(*sem):
    return pltpu.CompilerParams(dimension_semantics=sem, vmem_limit_bytes=VMEM_LIMIT)


def _tile(n, cap, unit=LANE):
    best = None
    for t in range(unit, min(n, cap) + 1, unit):
        if n % t == 0:
            best = t
    assert best is not None, (n, cap, unit)
    return best


def _rms(x, g):
    ms = jnp.mean(x * x, axis=-1, keepdims=True)
    return x * lax.rsqrt(ms + NORM_EPS) * g


def _dot(a, b):
    return jnp.dot(a, b, preferred_element_type=F32)


def _dot_nt(a, b):
    return lax.dot_general(a, b, (((1,), (1,)), ((), ())), preferred_element_type=F32)


def _norm_mm_kernel(h_ref, g_ref, w_ref, o_ref, hn_ref):
    @pl.when(pl.program_id(1) == 0)
    def _():
        hn_ref[...] = _rms(h_ref[...], g_ref[...]).astype(BF16)

    o_ref[...] = _dot(hn_ref[...], w_ref[...]).astype(o_ref.dtype)


def _norm_matmul(h, g, w, out_dtype, tm, tn):
    t, d = h.shape
    n = w.shape[1]
    return pl.pallas_call(
        _norm_mm_kernel,
        out_shape=jax.ShapeDtypeStruct((t, n), out_dtype),
        grid=(t // tm, n // tn),
        in_specs=[
            pl.BlockSpec((tm, d), lambda i, j: (i, 0)),
            pl.BlockSpec((1, d), lambda i, j: (0, 0)),
            pl.BlockSpec((d, tn), lambda i, j: (0, j)),
        ],
        out_specs=pl.BlockSpec((tm, tn), lambda i, j: (i, j)),
        scratch_shapes=[pltpu.VMEM((tm, d), BF16)],
        compiler_params=_params("parallel", "arbitrary"),
        name="norm_qkv",
    )(h, g, w)


def _glu_kernel(h_ref, g_ref, wv_ref, wg_ref, o_ref, hn_ref):
    @pl.when(pl.program_id(1) == 0)
    def _():
        hn_ref[...] = _rms(h_ref[...], g_ref[...]).astype(BF16)

    hn = hn_ref[...]
    o_ref[...] = _dot(hn, wv_ref[...]) * jax.nn.sigmoid(_dot(hn, wg_ref[...]))


def _norm_glu(h, g, w_val, w_gate, tm, tn):
    t, d = h.shape
    n = w_val.shape[1]
    return pl.pallas_call(
        _glu_kernel,
        out_shape=jax.ShapeDtypeStruct((t, n), F32),
        grid=(t // tm, n // tn),
        in_specs=[
            pl.BlockSpec((tm, d), lambda i, j: (i, 0)),
            pl.BlockSpec((1, d), lambda i, j: (0, 0)),
            pl.BlockSpec((d, tn), lambda i, j: (0, j)),
            pl.BlockSpec((d, tn), lambda i, j: (0, j)),
        ],
        out_specs=pl.BlockSpec((tm, tn), lambda i, j: (i, j)),
        scratch_shapes=[pltpu.VMEM((tm, d), BF16)],
        compiler_params=_params("parallel", "arbitrary"),
        name="norm_glu",
    )(h, g, w_val, w_gate)


def _mla_proj_kernel(h_ref, g_ref, wc_ref, gq_ref, gkv_ref, wuq_ref, wukv_ref, cos_ref, sin_ref,
                     q_ref, k_ref, v_ref):
    hn = _rms(h_ref[...], g_ref[...]).astype(BF16)
    c = _dot(hn, wc_ref[...])
    cos = cos_ref[...]
    sin = sin_ref[...]
    o_kr = MLA_Q_RANK + MLA_KV_RANK
    kr = (c[:, o_kr:o_kr + LANE] * cos + c[:, o_kr + LANE:o_kr + 2 * LANE] * sin).astype(BF16)
    cqn = _rms(c[:, :MLA_Q_RANK], gq_ref[...]).astype(BF16)
    ckvn = _rms(c[:, MLA_Q_RANK:o_kr], gkv_ref[...]).astype(BF16)
    for hd in range(MLA_HEADS):
        q = _dot(cqn, wuq_ref[:, hd * MLA_UQ_PAD:(hd + 1) * MLA_UQ_PAD]) * MLA_SCALE
        kv = _dot(ckvn, wukv_ref[:, hd * 2 * LANE:(hd + 1) * 2 * LANE])
        qo = hd * MLA_QK_PAD
        q_ref[:, qo:qo + LANE] = q[:, :LANE].astype(BF16)
        q_ref[:, qo + LANE:qo + 2 * LANE] = (q[:, LANE:2 * LANE] * cos + q[:, 2 * LANE:] * sin).astype(BF16)
        k_ref[:, qo:qo + LANE] = kv[:, :LANE].astype(BF16)
        k_ref[:, qo + LANE:qo + 2 * LANE] = kr
        v_ref[:, hd * LANE:(hd + 1) * LANE] = kv[:, LANE:].astype(BF16)


def _mla_proj(h, g, wc, gq, gkv, wuq, wukv, cos_t, sin_t, tm):
    t, d = h.shape
    lp = cos_t.shape[0]
    nper = lp // tm
    const = lambda i: (0, 0)
    return pl.pallas_call(
        _mla_proj_kernel,
        out_shape=(
            jax.ShapeDtypeStruct((t, MLA_HEADS * MLA_QK_PAD), BF16),
            jax.ShapeDtypeStruct((t, MLA_HEADS * MLA_QK_PAD), BF16),
            jax.ShapeDtypeStruct((t, MLA_HEADS * MLA_V_DIM), BF16),
        ),
        grid=(t // tm,),
        in_specs=[
            pl.BlockSpec((tm, d), lambda i: (i, 0)),
            pl.BlockSpec((1, d), const),
            pl.BlockSpec(wc.shape, const),
            pl.BlockSpec(gq.shape, const),
            pl.BlockSpec(gkv.shape, const),
            pl.BlockSpec(wuq.shape, const),
            pl.BlockSpec(wukv.shape, const),
            pl.BlockSpec((tm, LANE), lambda i: (i % nper, 0)),
            pl.BlockSpec((tm, LANE), lambda i: (i % nper, 0)),
        ],
        out_specs=(
            pl.BlockSpec((tm, MLA_HEADS * MLA_QK_PAD), lambda i: (i, 0)),
            pl.BlockSpec((tm, MLA_HEADS * MLA_QK_PAD), lambda i: (i, 0)),
            pl.BlockSpec((tm, MLA_HEADS * MLA_V_DIM), lambda i: (i, 0)),
        ),
        compiler_params=_params("parallel"),
        name="mla_proj",
    )(h, g, wc, gq, gkv, wuq, wukv, cos_t, sin_t)


CONV_HALO = 32
CONV_ROWS = 32
SUBLANES = 8


def _conv_kernel(cur_ref, halo_ref, w_ref, b_ref, g_ref, beta_ref, o_ref, xs_ref, xr_ref, *, n_real_tiles):
    t = pl.program_id(1)
    tc = cur_ref.shape[1]
    c = cur_ref.shape[2]

    @pl.when(t == n_real_tiles)
    def _():
        xs_ref[0:CONV_HALO, :] = jnp.zeros((CONV_HALO, c), F32)

    @pl.when(t == 0)
    def _():
        xs_ref[0:CONV_HALO - N_META, :] = jnp.zeros((CONV_HALO - N_META, c), F32)
        xs_ref[CONV_HALO - N_META:CONV_HALO, :] = halo_ref[0, 0:N_META, :]

    @pl.when((t > 0) & (t < n_real_tiles))
    def _():
        xs_ref[0:CONV_HALO, :] = halo_ref[0]

    xs_ref[CONV_HALO:, :] = cur_ref[0]

    n_shift = CONV_HALO + tc - SUBLANES
    for r in range(1, SUBLANES):
        xr_ref[r - 1, 0:n_shift, :] = xs_ref[r:r + n_shift, :]

    first_tap = CONV_HALO - (CONV_KERNEL - 1)
    for rc in range(tc // CONV_ROWS):
        base = rc * CONV_ROWS
        nt = CONV_ROWS // SUBLANES
        acc = jnp.broadcast_to(b_ref[...][None], (nt, SUBLANES, c))
        for k in range(CONV_KERNEL):
            shift = (first_tap + k) % SUBLANES
            lo = base + (first_tap + k) - shift
            if shift == 0:
                x = xs_ref[lo:lo + CONV_ROWS, :]
            else:
                x = xr_ref[shift - 1, lo:lo + CONV_ROWS, :]
            acc = acc + w_ref[k][None] * x.reshape(nt, SUBLANES, c)
        acc = acc.reshape(CONV_ROWS, c)
        mu = jnp.mean(acc, axis=-1, keepdims=True)
        ctr = acc - mu
        var = jnp.mean(ctr * ctr, axis=-1, keepdims=True)
        y = ctr * lax.rsqrt(var + NORM_EPS) * g_ref[...] + beta_ref[...]
        o_ref[0, base:base + CONV_ROWS, :] = (y * jax.nn.sigmoid(y)).astype(o_ref.dtype)


def _conv_branch(a, w, b, g, beta, seq):
    bsz, lp, c = a.shape
    tc = META_ROWS
    n_real = seq // tc
    hpt = tc // CONV_HALO

    def halo_map(bi, t):
        inside = (t > 0) & (t < n_real)
        return (bi, jnp.where(inside, t * hpt - 1, seq // CONV_HALO), 0)

    const = lambda bi, t: (0, 0)
    return pl.pallas_call(
        functools.partial(_conv_kernel, n_real_tiles=n_real),
        out_shape=jax.ShapeDtypeStruct((bsz, lp, c), BF16),
        grid=(bsz, n_real + 1),
        in_specs=[
            pl.BlockSpec((1, tc, c), lambda bi, t: (bi, t, 0)),
            pl.BlockSpec((1, CONV_HALO, c), halo_map),
            pl.BlockSpec((CONV_KERNEL, SUBLANES, c), lambda bi, t: (0, 0, 0)),
            pl.BlockSpec((SUBLANES, c), const),
            pl.BlockSpec((1, c), const),
            pl.BlockSpec((1, c), const),
        ],
        out_specs=pl.BlockSpec((1, tc, c), lambda bi, t: (bi, t, 0)),
        scratch_shapes=[pltpu.VMEM((CONV_HALO + tc, c), F32),
                        pltpu.VMEM((SUBLANES - 1, CONV_HALO + tc, c), F32)],
        compiler_params=_params("parallel", "arbitrary"),
        name="conv_ln_swish",
    )(a, a, jnp.broadcast_to(w[:, None, :], (CONV_KERNEL, SUBLANES, c)), jnp.broadcast_to(b, (SUBLANES, c)),
      g, beta)


def _softmax_pair_step(s1, s2, vl, vr, m_ref, l_ref, acc_ref):
    alphas, ps = [], []
    for i, s in enumerate((s1, s2)):
        m_prev = m_ref[i]
        m_new = jnp.maximum(m_prev, jnp.max(s, axis=1, keepdims=True))
        alpha = jnp.exp(m_prev - m_new)
        p = jnp.exp((s - jnp.concatenate([m_new] * (s.shape[1] // LANE), axis=1)).astype(BF16))
        l_ref[i] = alpha * l_ref[i] + jnp.sum(p.astype(F32), axis=1, keepdims=True)
        m_ref[i] = m_new
        alphas.append(alpha)
        ps.append(p)
    acc_ref[...] = (jnp.concatenate(alphas, axis=1) * acc_ref[...]
                    + _dot(ps[0], vl) + _dot(ps[1], vr))


def _split_values(v):
    lane = lax.broadcasted_iota(jnp.int32, v.shape, 1)
    zero = jnp.zeros_like(v)
    return jnp.where(lane < LANE, v, zero), jnp.where(lane < LANE, zero, v)


def _init_state(m_ref, l_ref, acc_ref):
    m_ref[...] = jnp.full(m_ref.shape, NEG_INF, F32)
    l_ref[...] = jnp.zeros(l_ref.shape, F32)
    acc_ref[...] = jnp.zeros(acc_ref.shape, F32)


def _split_diff_q(q):
    lane = lax.broadcasted_iota(jnp.int32, q.shape, 1)
    qs = q * jnp.asarray(DIFF_HEAD_DIM ** -0.5, q.dtype)
    zero = jnp.zeros_like(qs)
    return jnp.where(lane < DIFF_HEAD_DIM, qs, zero), jnp.where(lane < DIFF_HEAD_DIM, zero, qs)


def _diff_lambda(lq1, lk1, lq2, lk2, lam_init):
    return (jnp.exp(jnp.sum(lq1 * lk1, axis=-1, keepdims=True))
            - jnp.exp(jnp.sum(lq2 * lk2, axis=-1, keepdims=True)) + lam_init)


def _diff_finish(o1, o2, lam, subln, lam_init):
    o = o1 - lam * o2
    return _rms(o, subln) * (1.0 - lam_init)


def _causal_sweep(qi, scores, update):
    npairs = jnp.where(qi > 0, lax.shift_right_logical(qi - 1, 1), 0)
    scores(0, 0)

    def pair(u, carry):
        t = 2 * u
        scores(t + 1, 1)
        update(t, 0, False)
        scores(t + 2, 0)
        update(t + 1, 1, False)
        return carry

    lax.fori_loop(0, npairs, pair, 0)

    t0 = 2 * npairs
    scores(jnp.minimum(t0 + 1, qi), 1)
    update(t0, 0, True)

    @pl.when(t0 + 1 <= qi)
    def _():
        scores(jnp.minimum(t0 + 2, qi), 0)
        update(t0 + 1, 1, True)

    @pl.when(t0 + 2 <= qi)
    def _():
        update(t0 + 2, 0, True)


def _near_tile(t, qi):
    return jnp.clip(t - qi + 2, 0, 2)


def _diff_kernel(q_ref, k_ref, v_ref, km_ref, vm_ref, bn_ref, bm_ref,
                 lq1_ref, lk1_ref, lq2_ref, lk2_ref, sub_ref, li_ref, base_ref,
                 o_ref, m_ref, l_ref, acc_ref, s_ref, vl_ref, vr_ref):
    del base_ref
    qi = pl.program_id(2)
    tq = q_ref.shape[1]
    q1, q2 = _split_diff_q(q_ref[0])
    _init_state(m_ref, l_ref, acc_ref)

    def widen(v):
        return _split_values(jnp.concatenate([v, v], axis=1))

    @pl.when(qi == 0)
    def _():
        vl, vr = widen(v_ref[0])
        vl_ref[...] = vl
        vr_ref[...] = vr

    km = km_ref[0]
    vml, vmr = widen(vm_ref[0])
    bm = bm_ref[0, 0]
    _softmax_pair_step(_dot_nt(q1, km) + bm, _dot_nt(q2, km) + bm, vml, vmr, m_ref, l_ref, acc_ref)

    def scores(t, slot):
        k = k_ref[0, pl.ds(pl.multiple_of(t * tq, tq), tq), :]
        s_ref[slot, 0] = _dot_nt(q1, k)
        s_ref[slot, 1] = _dot_nt(q2, k)

    def update(t, slot, biased):
        rows = pl.ds(pl.multiple_of(t * tq, tq), tq)
        s1 = s_ref[slot, 0]
        s2 = s_ref[slot, 1]
        if biased:
            bias = bn_ref[0, _near_tile(t, qi)]
            s1 = s1 + bias
            s2 = s2 + bias
        _softmax_pair_step(s1, s2, vl_ref[rows, :], vr_ref[rows, :], m_ref, l_ref, acc_ref)

    _causal_sweep(qi, scores, update)

    li = li_ref[...]
    lam = _diff_lambda(lq1_ref[...], lk1_ref[...], lq2_ref[...], lk2_ref[...], li)
    o = _diff_finish(acc_ref[:, :DIFF_V_DIM] / l_ref[0], acc_ref[:, DIFF_V_DIM:] / l_ref[1], lam,
                     sub_ref[...], li)
    o_ref[0] = o.astype(o_ref.dtype)


def _diff_meta_kernel(q_ref, k_ref, v_ref, bm_ref, lq1_ref, lk1_ref, lq2_ref, lk2_ref, sub_ref, li_ref,
                      prev_ref, o_ref):
    del prev_ref
    li = li_ref[...]
    lam = _diff_lambda(lq1_ref[...], lk1_ref[...], lq2_ref[...], lk2_ref[...], li)
    for hd in range(DIFF_HEADS):
        sl = slice(hd * LANE, (hd + 1) * LANE)
        q1, q2 = _split_diff_q(q_ref[0, :, sl])
        k = k_ref[0, :, sl]
        v = v_ref[0, :, sl]
        outs = []
        for qz in (q1, q2):
            s = _dot_nt(qz, k) + bm_ref[hd]
            p = jnp.exp(s - jnp.max(s, axis=1, keepdims=True))
            outs.append(_dot(p.astype(BF16), v) / jnp.sum(p, axis=1, keepdims=True))
        o_ref[0, :, sl] = _diff_finish(outs[0], outs[1], lam, sub_ref[...], li).astype(o_ref.dtype)


def _diff_attention(qkv, bias, lq1, lk1, lq2, lk2, subln, lam_init, seq, tq):
    bsz, lp, _ = qkv.shape
    nq = seq // tq
    mb = seq // META_ROWS
    hq = DIFF_HEADS
    vec = lambda n: pl.BlockSpec((1, n), lambda *_: (0, 0))
    small = [vec(DIFF_HEAD_DIM)] * 4 + [vec(DIFF_V_DIM), vec(1)]
    smalls = (lq1, lk1, lq2, lk2, subln, lam_init)
    out = pl.pallas_call(
        _diff_kernel,
        out_shape=jax.ShapeDtypeStruct((bsz, lp, hq * DIFF_V_DIM), BF16),
        grid=(bsz, hq, nq),
        in_specs=[
            pl.BlockSpec((1, tq, LANE), lambda b, h, i: (b, i, h)),
            pl.BlockSpec((1, seq, LANE), lambda b, h, i: (b, 0, hq + h)),
            pl.BlockSpec((1, seq, LANE), lambda b, h, i: (b, 0, 2 * hq + h)),
            pl.BlockSpec((1, META_ROWS, LANE), lambda b, h, i: (b, mb, hq + h)),
            pl.BlockSpec((1, META_ROWS, LANE), lambda b, h, i: (b, mb, 2 * hq + h)),
            pl.BlockSpec((1, 3, tq, tq), lambda b, h, i: (h, 0, 0, 0)),
            pl.BlockSpec((1, 1, tq, META_ROWS), lambda b, h, i: (h, jnp.minimum(i, 1), 0, 0)),
        ] + small + [pl.BlockSpec(memory_space=pl.ANY)],
        out_specs=pl.BlockSpec((1, tq, LANE), lambda b, h, i: (b, i, h)),
        input_output_aliases={13: 0},
        scratch_shapes=[pltpu.VMEM((2, tq, LANE), F32), pltpu.VMEM((2, tq, LANE), F32),
                        pltpu.VMEM((tq, 2 * DIFF_V_DIM), F32), pltpu.VMEM((2, 2, tq, tq), F32),
                        pltpu.VMEM((seq, 2 * DIFF_V_DIM), BF16), pltpu.VMEM((seq, 2 * DIFF_V_DIM), BF16)],
        compiler_params=_params("parallel", "parallel", "arbitrary"),
        name="diff_attention",
    )(qkv, qkv, qkv, qkv, qkv, bias["near"], bias["meta_keys"], *smalls,
      jnp.zeros((bsz, lp, hq * DIFF_V_DIM), BF16))

    wide = hq * LANE
    return pl.pallas_call(
        _diff_meta_kernel,
        out_shape=jax.ShapeDtypeStruct(out.shape, out.dtype),
        grid=(bsz,),
        in_specs=[
            pl.BlockSpec((1, META_ROWS, wide), lambda b: (b, mb, 0)),
            pl.BlockSpec((1, META_ROWS, wide), lambda b: (b, mb, 1)),
            pl.BlockSpec((1, META_ROWS, wide), lambda b: (b, mb, 2)),
            pl.BlockSpec(bias["meta_meta"].shape, lambda b: (0, 0, 0)),
        ] + small + [pl.BlockSpec(memory_space=pl.ANY)],
        out_specs=pl.BlockSpec((1, META_ROWS, wide), lambda b: (b, mb, 0)),
        input_output_aliases={10: 0},
        compiler_params=_params("parallel"),
        name="diff_attention_meta",
    )(qkv, qkv, qkv, bias["meta_meta"], *smalls, out)


def _mla_kernel(q_ref, k_ref, v_ref, km_ref, vm_ref, md_ref, mm_ref, base_ref, o_ref, m_ref, l_ref, acc_ref,
                s_ref, vl_ref, vr_ref):
    del base_ref
    qi = pl.program_id(2)
    tq = q_ref.shape[1]
    qa = q_ref[0, :, :MLA_QK_PAD]
    qb = q_ref[0, :, MLA_QK_PAD:]
    _init_state(m_ref, l_ref, acc_ref)

    @pl.when(qi == 0)
    def _():
        vl, vr = _split_values(v_ref[0])
        vl_ref[...] = vl
        vr_ref[...] = vr

    km = km_ref[0]
    vml, vmr = _split_values(vm_ref[0])
    mm = mm_ref[...]
    _softmax_pair_step(_dot_nt(qa, km[:, :MLA_QK_PAD]) + mm,
                       _dot_nt(qb, km[:, MLA_QK_PAD:]) + mm, vml, vmr, m_ref, l_ref, acc_ref)

    def scores(t, slot):
        rows = pl.ds(pl.multiple_of(t * tq, tq), tq)
        s_ref[slot, 0] = _dot_nt(qa, k_ref[0, rows, :MLA_QK_PAD])
        s_ref[slot, 1] = _dot_nt(qb, k_ref[0, rows, MLA_QK_PAD:])

    def update(t, slot, masked):
        rows = pl.ds(pl.multiple_of(t * tq, tq), tq)
        s1 = s_ref[slot, 0]
        s2 = s_ref[slot, 1]
        if masked:
            mask = md_ref[_near_tile(t, qi)]
            s1 = s1 + mask
            s2 = s2 + mask
        _softmax_pair_step(s1, s2, vl_ref[rows, :], vr_ref[rows, :], m_ref, l_ref, acc_ref)

    _causal_sweep(qi, scores, update)
    o_ref[0] = (acc_ref[...] / jnp.concatenate([l_ref[0], l_ref[1]], axis=1)).astype(o_ref.dtype)


def _mla_meta_kernel(q_ref, k_ref, v_ref, mm_ref, prev_ref, o_ref):
    del prev_ref
    for hd in range(MLA_HEADS):
        qk = slice(hd * MLA_QK_PAD, (hd + 1) * MLA_QK_PAD)
        vs = slice(hd * MLA_V_DIM, (hd + 1) * MLA_V_DIM)
        s = _dot_nt(q_ref[0, :, qk], k_ref[0, :, qk]) + mm_ref[...]
        p = jnp.exp(s - jnp.max(s, axis=1, keepdims=True))
        o = _dot(p.astype(BF16), v_ref[0, :, vs]) / jnp.sum(p, axis=1, keepdims=True)
        o_ref[0, :, vs] = o.astype(o_ref.dtype)


def _mla_attention(qm, km, vm, masks, seq, tq):
    bsz, lp, _ = qm.shape
    nq = seq // tq
    mb = seq // META_ROWS
    out = pl.pallas_call(
        _mla_kernel,
        out_shape=jax.ShapeDtypeStruct((bsz, lp, MLA_HEADS * MLA_V_DIM), BF16),
        grid=(bsz, MLA_HEADS // 2, nq),
        in_specs=[
            pl.BlockSpec((1, tq, 2 * MLA_QK_PAD), lambda b, h, i: (b, i, h)),
            pl.BlockSpec((1, seq, 2 * MLA_QK_PAD), lambda b, h, i: (b, 0, h)),
            pl.BlockSpec((1, seq, 2 * MLA_V_DIM), lambda b, h, i: (b, 0, h)),
            pl.BlockSpec((1, META_ROWS, 2 * MLA_QK_PAD), lambda b, h, i: (b, mb, h)),
            pl.BlockSpec((1, META_ROWS, 2 * MLA_V_DIM), lambda b, h, i: (b, mb, h)),
            pl.BlockSpec((3, tq, tq), lambda b, h, i: (0, 0, 0)),
            pl.BlockSpec((1, META_ROWS), lambda b, h, i: (0, 0)),
            pl.BlockSpec(memory_space=pl.ANY),
        ],
        out_specs=pl.BlockSpec((1, tq, 2 * MLA_V_DIM), lambda b, h, i: (b, i, h)),
        input_output_aliases={7: 0},
        scratch_shapes=[pltpu.VMEM((2, tq, LANE), F32), pltpu.VMEM((2, tq, LANE), F32),
                        pltpu.VMEM((tq, 2 * MLA_V_DIM), F32), pltpu.VMEM((2, 2, tq, tq), F32),
                        pltpu.VMEM((seq, 2 * MLA_V_DIM), BF16), pltpu.VMEM((seq, 2 * MLA_V_DIM), BF16)],
        compiler_params=_params("parallel", "parallel", "arbitrary"),
        name="mla_attention",
    )(qm, km, vm, km, vm, masks["near"], masks["meta_cols"],
      jnp.zeros((bsz, lp, MLA_HEADS * MLA_V_DIM), BF16))

    return pl.pallas_call(
        _mla_meta_kernel,
        out_shape=jax.ShapeDtypeStruct(out.shape, out.dtype),
        grid=(bsz,),
        in_specs=[
            pl.BlockSpec((1, META_ROWS, MLA_HEADS * MLA_QK_PAD), lambda b: (b, mb, 0)),
            pl.BlockSpec((1, META_ROWS, MLA_HEADS * MLA_QK_PAD), lambda b: (b, mb, 0)),
            pl.BlockSpec((1, META_ROWS, MLA_HEADS * MLA_V_DIM), lambda b: (b, mb, 0)),
            pl.BlockSpec((1, META_ROWS), lambda b: (0, 0)),
            pl.BlockSpec(memory_space=pl.ANY),
        ],
        out_specs=pl.BlockSpec((1, META_ROWS, MLA_HEADS * MLA_V_DIM), lambda b: (b, mb, 0)),
        input_output_aliases={4: 0},
        compiler_params=_params("parallel"),
        name="mla_attention_meta",
    )(qm, km, vm, masks["meta_cols"], out)


def _mix_kernel(h_ref, g_ref, xa_ref, xb_ref, xc_ref, ga_ref, gb_ref, gc_ref, wa_ref, wb_ref, wc_ref,
                o_ref, hn_ref):
    @pl.when(pl.program_id(1) == 0)
    def _():
        hn_ref[...] = _rms(h_ref[...], g_ref[...]).astype(BF16)

    hn = hn_ref[...]
    mixed = jax.nn.sigmoid(_dot(hn, ga_ref[...])) * _dot(xa_ref[...], wa_ref[...])
    mixed = mixed + jax.nn.sigmoid(_dot(hn, gb_ref[...])) * _dot(xb_ref[...], wb_ref[...])
    mixed = mixed + jax.nn.sigmoid(_dot(hn, gc_ref[...])) * _dot(xc_ref[...], wc_ref[...])
    o_ref[...] = mixed.astype(o_ref.dtype)


def _gated_mix(h, g, xa, xb, xc, w_gates, wa, wb, wc, tm, tn):
    t, d = h.shape
    nj = d // tn
    branch = lambda x: pl.BlockSpec((tm, x.shape[1]), lambda i, j: (i, 0))
    gate = lambda n: pl.BlockSpec((d, tn), lambda i, j: (0, n * nj + j))
    wbr = lambda w: pl.BlockSpec((w.shape[0], tn), lambda i, j: (0, j))
    return pl.pallas_call(
        _mix_kernel,
        out_shape=jax.ShapeDtypeStruct((t, d), BF16),
        grid=(t // tm, nj),
        in_specs=[
            pl.BlockSpec((tm, d), lambda i, j: (i, 0)),
            pl.BlockSpec((1, d), lambda i, j: (0, 0)),
            branch(xa), branch(xb), branch(xc),
            gate(0), gate(1), gate(2),
            wbr(wa), wbr(wb), wbr(wc),
        ],
        out_specs=pl.BlockSpec((tm, tn), lambda i, j: (i, j)),
        scratch_shapes=[pltpu.VMEM((tm, d), BF16)],
        compiler_params=_params("parallel", "arbitrary"),
        name="gated_mix",
    )(h, g, xa, xb, xc, w_gates, w_gates, w_gates, wa, wb, wc)


def _residual_mm_kernel(h_ref, x_ref, w_ref, o_ref):
    o_ref[...] = h_ref[...] + _dot(x_ref[...], w_ref[...])


def _residual_matmul(h, x, w, tm, tn):
    t, d = h.shape
    k = x.shape[1]
    return pl.pallas_call(
        _residual_mm_kernel,
        out_shape=jax.ShapeDtypeStruct((t, d), F32),
        grid=(t // tm, d // tn),
        in_specs=[
            pl.BlockSpec((tm, tn), lambda i, j: (i, j)),
            pl.BlockSpec((tm, k), lambda i, j: (i, 0)),
            pl.BlockSpec((k, tn), lambda i, j: (0, j)),
        ],
        out_specs=pl.BlockSpec((tm, tn), lambda i, j: (i, j)),
        compiler_params=_params("parallel", "arbitrary"),
        name="out_proj",
    )(h, x, w)


def _ffn_kernel(h_ref, g_ref, wg_ref, wu_ref, wd_ref, o_ref, hn_ref):
    f = pl.program_id(1)

    @pl.when(f == 0)
    def _():
        h = h_ref[...]
        hn_ref[...] = _rms(h, g_ref[...]).astype(BF16)
        o_ref[...] = h

    hn = hn_ref[...]
    gate = _dot(hn, wg_ref[...])
    act = (gate * jax.nn.sigmoid(gate) * _dot(hn, wu_ref[...])).astype(BF16)
    o_ref[...] += _dot(act, wd_ref[...])


def _ffn(h, g, wg, wu, wd, tm, tf):
    t, d = h.shape
    dff = wg.shape[1]
    return pl.pallas_call(
        _ffn_kernel,
        out_shape=jax.ShapeDtypeStruct((t, d), F32),
        grid=(t // tm, dff // tf),
        in_specs=[
            pl.BlockSpec((tm, d), lambda i, f: (i, 0)),
            pl.BlockSpec((1, d), lambda i, f: (0, 0)),
            pl.BlockSpec((d, tf), lambda i, f: (0, f)),
            pl.BlockSpec((d, tf), lambda i, f: (0, f)),
            pl.BlockSpec((tf, d), lambda i, f: (f, 0)),
        ],
        out_specs=pl.BlockSpec((tm, d), lambda i, f: (i, 0)),
        scratch_shapes=[pltpu.VMEM((tm, d), BF16)],
        compiler_params=_params("parallel", "arbitrary"),
        name="swiglu_ffn",
    )(h, g, wg, wu, wd)


def _final_norm_kernel(h_ref, g_ref, o_ref):
    o_ref[0] = _rms(h_ref[0], g_ref[...])


def _final_norm(h3, g, seq, tr):
    bsz, _, d = h3.shape
    return pl.pallas_call(
        _final_norm_kernel,
        out_shape=jax.ShapeDtypeStruct((bsz, seq, d), F32),
        grid=(bsz, seq // tr),
        in_specs=[
            pl.BlockSpec((1, tr, d), lambda b, i: (b, i, 0)),
            pl.BlockSpec((1, d), lambda b, i: (0, 0)),
        ],
        out_specs=pl.BlockSpec((1, tr, d), lambda b, i: (b, i, 0)),
        compiler_params=_params("parallel", "parallel"),
        name="final_norm",
    )(h3, g)


def _t5_bucket(rel):
    nb = REL_BUCKETS // 2
    ret = jnp.where(rel > 0, nb, 0)
    n = jnp.abs(rel)
    max_exact = nb // 2
    nf = jnp.maximum(n, 1).astype(F32)
    large = max_exact + (jnp.log(nf / max_exact) / math.log(REL_MAX_DIST / max_exact)
                         * (nb - max_exact)).astype(jnp.int32)
    large = jnp.minimum(large, nb - 1)
    return ret + jnp.where(n < max_exact, n, large)


def _position_tables(rel_bias, seq, tq):
    far = rel_bias[_t5_bucket(jnp.asarray(-REL_MAX_DIST, jnp.int32))]
    table = (rel_bias - far[None, :]).astype(F32)

    def bias(rel, visible):
        bucket = _t5_bucket(rel)[None]
        t = jnp.zeros((DIFF_HEADS,) + rel.shape, F32)
        for b in range(REL_BUCKETS):
            t = t + jnp.where(bucket == b, table[b][:, None, None], 0.0)
        return jnp.where(visible[None], t, NEG_INF)

    r = jnp.arange(tq, dtype=jnp.int32)[:, None]
    c = jnp.arange(tq, dtype=jnp.int32)[None, :]
    chunk_vis = (c // CHUNK) <= (r // CHUNK)
    all_vis = jnp.ones((tq, tq), bool)
    m = jnp.arange(META_ROWS, dtype=jnp.int32)
    meta_vis = jnp.broadcast_to((m < N_META)[None, :], (tq, META_ROWS))
    diff = {
        "near": jnp.stack([jnp.zeros((DIFF_HEADS, tq, tq), F32), bias(c - tq - r, all_vis),
                           bias(c - r, chunk_vis)], axis=1),
        "meta_keys": jnp.stack([bias(m[None, :] - N_META - r, meta_vis),
                                bias(jnp.full((tq, META_ROWS), -REL_MAX_DIST - tq, jnp.int32), meta_vis)],
                               axis=1),
        "meta_meta": bias(m[None, :] - m[:, None],
                          jnp.broadcast_to((m < N_META)[None, :], (META_ROWS, META_ROWS))),
    }
    mla = {
        "near": jnp.stack([jnp.zeros((tq, tq), F32), jnp.zeros((tq, tq), F32),
                           jnp.where(chunk_vis, 0.0, NEG_INF).astype(F32)]),
        "meta_cols": jnp.where(m < N_META, 0.0, NEG_INF).astype(F32)[None, :],
    }
    return diff, mla


def _rope_tables(seq):
    half = MLA_ROPE_DIM // 2
    row = jnp.arange(seq + META_ROWS, dtype=jnp.int32)
    pos = jnp.where(row < seq, row + N_META, row - seq)
    inv = ROPE_THETA ** (-jnp.arange(half, dtype=F32) / half)
    ang = pos.astype(F32)[:, None] * inv[None, :]
    cos, sin = jnp.cos(ang), jnp.sin(ang)
    pad = jnp.zeros((seq + META_ROWS, LANE - MLA_ROPE_DIM), F32)
    return (jnp.concatenate([cos, cos, pad], axis=1), jnp.concatenate([-sin, sin, pad], axis=1))


def _swap_halves(w):
    half = w.shape[-1] // 2
    return jnp.concatenate([w[..., half:], w[..., :half]], axis=-1)


def _lane_pad(w):
    return jnp.concatenate([w, jnp.zeros(w.shape[:-1] + (LANE - w.shape[-1],), w.dtype)], axis=-1)


def kernel(x, meta_tokens, rel_bias, norm_mix, w_in, conv_w, conv_b, conv_ln_g, conv_ln_b, w_br_conv,
           diff_lq1, diff_lk1, diff_lq2, diff_lk2, diff_subln, w_br_diff, mla_q_norm, w_uq, mla_kv_norm,
           w_ukv, w_br_mla, w_out, norm_ffn, w_ffn_gate, w_ffn_up, w_ffn_down, final_norm):
    bsz, seq, d = x.shape
    depth = w_in.shape[0]
    dff = w_ffn_gate.shape[-1]
    lp = seq + META_ROWS
    t = bsz * lp
    assert seq % META_ROWS == 0 and d % LANE == 0

    tq = _tile(seq, 512)
    tm = _tile(lp, 640)
    tn = _tile(d, 1024)
    tf = _tile(dff, 512)
    tn_mix = _tile(d, 512)
    tn_out = _tile(d, 2048)
    tn_qkv = _tile(OFF_CQ - OFF_BQ, 1536)

    def layer_weights(l):
        wi = w_in[l]
        w_kr = wi[:, OFF_CR:OFF_G]
        uq = w_uq[l].reshape(MLA_Q_RANK, MLA_HEADS, MLA_NOPE_DIM + MLA_ROPE_DIM)
        uq_rope = uq[..., MLA_NOPE_DIM:]
        return dict(
            val=wi[:, OFF_A:OFF_A + CONV_WIDTH].astype(BF16),
            gate=wi[:, OFF_A + CONV_WIDTH:OFF_BQ].astype(BF16),
            qkv=wi[:, OFF_BQ:OFF_CQ].astype(BF16),
            c=jnp.concatenate([wi[:, OFF_CQ:OFF_CR], _lane_pad(w_kr), _lane_pad(_swap_halves(w_kr))],
                              axis=-1).astype(BF16),
            gates=wi[:, OFF_G:].astype(BF16),
            uq=jnp.concatenate([uq[..., :MLA_NOPE_DIM], _lane_pad(uq_rope), _lane_pad(_swap_halves(uq_rope))],
                               axis=-1).reshape(MLA_Q_RANK, MLA_HEADS * MLA_UQ_PAD).astype(BF16),
            ukv=w_ukv[l].astype(BF16),
            bra=w_br_conv[l].astype(BF16), brb=w_br_diff[l].astype(BF16), brc=w_br_mla[l].astype(BF16),
            out=w_out[l].astype(BF16),
            fg=w_ffn_gate[l].astype(BF16), fu=w_ffn_up[l].astype(BF16), fd=w_ffn_down[l].astype(BF16),
        )

    diff_tabs, mla_tabs = _position_tables(rel_bias, seq, tq)
    cos_t, sin_t = _rope_tables(seq)

    row = lambda v: v.reshape(1, -1)
    h = jnp.concatenate([x, jnp.broadcast_to(meta_tokens[None].astype(x.dtype), (bsz, N_META, d)),
                         jnp.zeros((bsz, META_ROWS - N_META, d), x.dtype)], axis=1).reshape(t, d)

    for l in range(depth):
        lam_init = jnp.full((1, 1), 0.8 - 0.6 * math.exp(-0.3 * l), F32)
        g_mix = row(norm_mix[l])
        w = layer_weights(l)
        a = _norm_glu(h, g_mix, w["val"], w["gate"], tm, tn)
        qkv = _norm_matmul(h, g_mix, w["qkv"], BF16, tm, tn_qkv)
        qm, km, vm = _mla_proj(h, g_mix, w["c"], row(mla_q_norm[l]), row(mla_kv_norm[l]), w["uq"],
                               w["ukv"], cos_t, sin_t, tm)
        ya = _conv_branch(a.reshape(bsz, lp, CONV_WIDTH), conv_w[l], row(conv_b[l]), row(conv_ln_g[l]),
                          row(conv_ln_b[l]), seq)
        yb = _diff_attention(qkv.reshape(bsz, lp, -1), diff_tabs, row(diff_lq1[l]), row(diff_lk1[l]),
                             row(diff_lq2[l]), row(diff_lk2[l]), row(diff_subln[l]), lam_init, seq, tq)
        yc = _mla_attention(qm.reshape(bsz, lp, -1), km.reshape(bsz, lp, -1), vm.reshape(bsz, lp, -1),
                            mla_tabs, seq, tq)
        mixed = _gated_mix(h, g_mix, ya.reshape(t, -1), yb.reshape(t, -1), yc.reshape(t, -1),
                           w["gates"], w["bra"], w["brb"], w["brc"], tm, tn_mix)
        h = _residual_matmul(h, mixed, w["out"], tm, tn_out)
        h = _ffn(h, row(norm_ffn[l]), w["fg"], w["fu"], w["fd"], tm, tf)

    return _final_norm(h.reshape(bsz, lp, d), row(final_norm), seq, _tile(seq, 512))
```
